```python
import math
import jax
import jax.numpy as jnp
from jax import lax
import numpy as np

D_MODEL = 1024
BATCH = 32
SEQ = 2048
DEPTH = 2
DEC_BATCH = 128
DEC_SEQ = 1
PAST_LEN = 16384
PAGE_SIZE = 128

RMS_EPS = 1e-6
D_MIX = D_MODEL
A_HEADS = 4
A_HEAD_DIM = D_MIX // 16
A_WIDTH = A_HEADS * A_HEAD_DIM
CHUNK = 128
SSM_CH = 16
SSM_WIDTH = D_MIX // 4
SSM_GROUPS = SSM_WIDTH // SSM_CH
SSM_STATE = 64
MLA_HEADS = 8
QK_NOPE = 64
QK_ROPE = 32
V_DIM = D_MIX // 2 // MLA_HEADS
Q_RANK = D_MODEL // 4
KV_RANK = D_MODEL // 8
MLA_WIDTH = MLA_HEADS * V_DIM
ROPE_THETA = 10000.0
Q_BLOCK = 128
ATTN_SCALE = (QK_NOPE + QK_ROPE) ** -0.5
IN_WIDTH = 2 * A_WIDTH + SSM_WIDTH + Q_RANK + KV_RANK + QK_ROPE
N_GROUPS = 4
EXPERTS_PER_GROUP = 8
N_EXPERTS = N_GROUPS * EXPERTS_PER_GROUP
TOP_K_IN_GROUP = 2
D_EXPERT = D_MODEL // 2
MOE_BLOCK = 128

kernel_name = 'hymba_gmlp_s5_mla_hier_moe_step'


def rmsnorm(x, g):
    xf = x.astype(jnp.float32)
    xf = xf * lax.rsqrt(jnp.mean(xf * xf, axis=-1, keepdims=True) + RMS_EPS)
    return xf.astype(x.dtype) * g


def chunk_spatial_gating(u, v, sp_w, sp_b):
    B, S, _ = u.shape
    cl = min(CHUNK, S)
    n = S // cl
    vh = v.reshape(B, n, cl, A_HEADS, A_HEAD_DIM)
    w = jnp.tril(sp_w[:, :cl, :cl])
    gate = jnp.einsum('hts,bnshd->bnthd', w, vh) + jnp.transpose(sp_b[:, :cl])[None, None, :, :, None]
    return u * gate.reshape(B, S, A_WIDTH)


def diag_scan(a_re, a_im, bu_re, bu_im, x0_re, x0_im):
    def combine(e1, e2):
        a1r, a1i, b1r, b1i = e1
        a2r, a2i, b2r, b2i = e2
        return (a2r * a1r - a2i * a1i, a2r * a1i + a2i * a1r,
                a2r * b1r - a2i * b1i + b2r, a2r * b1i + a2i * b1r + b2i)
    ar = jnp.broadcast_to(a_re, bu_re.shape)
    ai = jnp.broadcast_to(a_im, bu_im.shape)
    pr, pim, hr, hi = lax.associative_scan(combine, (ar, ai, bu_re, bu_im), axis=1)
    x0r = x0_re[:, None]
    x0i = x0_im[:, None]
    return hr + pr * x0r - pim * x0i, hi + pr * x0i + pim * x0r


def s5_mixer(u, lam_re, lam_im, b_re, b_im, c_re, c_im, d, log_step, glu_w, glu_b, x0_re, x0_im):
    B, S, _ = u.shape
    ug = u.reshape(B, S, SSM_GROUPS, SSM_CH)
    dt = jnp.exp(log_step)[:, None]
    mag = jnp.exp(lam_re * dt)
    abar_re = mag * jnp.cos(lam_im * dt)
    abar_im = mag * jnp.sin(lam_im * dt)
    den = lam_re * lam_re + lam_im * lam_im
    nr = abar_re - 1.0
    fr = (nr * lam_re + abar_im * lam_im) / den
    fi = (abar_im * lam_re - nr * lam_im) / den
    bbar_re = fr[..., None] * b_re - fi[..., None] * b_im
    bbar_im = fr[..., None] * b_im + fi[..., None] * b_re
    bu_re = jnp.einsum('gpc,bsgc->bsgp', bbar_re, ug)
    bu_im = jnp.einsum('gpc,bsgc->bsgp', bbar_im, ug)
    x_re, x_im = diag_scan(abar_re, abar_im, bu_re, bu_im, x0_re, x0_im)
    y = jnp.einsum('gcp,bsgp->bsgc', c_re, x_re) - jnp.einsum('gcp,bsgp->bsgc', c_im, x_im)
    y = y.reshape(B, S, SSM_WIDTH) + d * u
    z = jax.nn.gelu(y)
    return z * jax.nn.sigmoid(z @ glu_w + glu_b), x_re[:, -1], x_im[:, -1]


def rope_tables(pos0, S):
    pos = (pos0 + jnp.arange(S)).astype(jnp.float32)
    inv = jnp.power(ROPE_THETA, -jnp.arange(0, QK_ROPE, 2, dtype=jnp.float32) / QK_ROPE)
    ang = pos[:, None] * inv[None, :]
    return jnp.cos(ang), jnp.sin(ang)


def apply_rope(x, cos, sin):
    half = x.shape[-1] // 2
    x1, x2 = x[..., :half], x[..., half:]
    cos = cos.astype(x.dtype)
    sin = sin.astype(x.dtype)
    return jnp.concatenate([x1 * cos - x2 * sin, x2 * cos + x1 * sin], axis=-1)


def latent_scores(q_lat, q_pe, ckv, kpe):
    s = jnp.einsum('bqhr,bkr->bhqk', q_lat, ckv) + jnp.einsum('bqhp,bkp->bhqk', q_pe, kpe)
    return s.astype(jnp.float32) * ATTN_SCALE


def prompt_attention(q_lat, q_pe, ckv, kpe):
    B, S, H, R = q_lat.shape
    nb = S // Q_BLOCK
    qb = jnp.transpose(q_lat.reshape(B, nb, Q_BLOCK, H, R), (1, 0, 2, 3, 4))
    pb = jnp.transpose(q_pe.reshape(B, nb, Q_BLOCK, H, QK_ROPE), (1, 0, 2, 3, 4))
    kpos = jnp.arange(S)

    def one_block(args):
        i, ql, qp = args
        s = latent_scores(ql, qp, ckv, kpe)
        qpos = i * Q_BLOCK + jnp.arange(Q_BLOCK)
        s = jnp.where(kpos[None, :] <= qpos[:, None], s, -jnp.inf)
        p = jax.nn.softmax(s, axis=-1).astype(ckv.dtype)
        return jnp.einsum('bhqk,bkr->bqhr', p, ckv)

    o = lax.map(one_block, (jnp.arange(nb), qb, pb))
    return jnp.transpose(o, (1, 0, 2, 3, 4)).reshape(B, S, H, R)


def sample_attention(q_lat, q_pe, ckv_past, kpe_past, ckv_new, kpe_new):
    S = q_lat.shape[1]
    P = ckv_past.shape[1]
    s_past = latent_scores(q_lat, q_pe, ckv_past, kpe_past)
    s_new = latent_scores(q_lat, q_pe, ckv_new, kpe_new)
    causal = jnp.tril(jnp.ones((S, S), dtype=bool))
    s_new = jnp.where(causal, s_new, -jnp.inf)
    p = jax.nn.softmax(jnp.concatenate([s_past, s_new], axis=-1), axis=-1).astype(ckv_new.dtype)
    return (jnp.einsum('bhqk,bkr->bqhr', p[..., :P], ckv_past)
            + jnp.einsum('bhqk,bkr->bqhr', p[..., P:], ckv_new))


def mla(c_q, c_kv, k_pe, pos0, lp, past):
    B, S, _ = c_q.shape
    cq = rmsnorm(c_q, lp['q_norm_g'])
    q = (cq @ lp['w_uq']).reshape(B, S, MLA_HEADS, QK_NOPE + QK_ROPE)
    q_nope, q_pe = q[..., :QK_NOPE], q[..., QK_NOPE:]
    ckv = rmsnorm(c_kv, lp['kv_norm_g'])
    cos, sin = rope_tables(pos0, S)
    q_pe = apply_rope(q_pe, cos[:, None, :], sin[:, None, :])
    kpe = apply_rope(k_pe, cos, sin)
    q_lat = jnp.einsum('bshd,rhd->bshr', q_nope, lp['w_uk'])
    if past is None:
        o = prompt_attention(q_lat, q_pe, ckv, kpe)
    else:
        o = sample_attention(q_lat, q_pe, past[0], past[1], ckv, kpe)
    y = jnp.einsum('bshr,rhd->bshd', o, lp['w_uv']).reshape(B, S, MLA_WIDTH)
    return y, ckv, kpe


def token_mixers(h, pos0, lp, x0_re, x0_im, past):
    o1 = A_WIDTH
    o2 = o1 + A_WIDTH
    o3 = o2 + SSM_WIDTH
    o4 = o3 + Q_RANK
    o5 = o4 + KV_RANK
    proj = h @ lp['w_in']
    a_u, a_v, b_u, c_q, c_kv, k_pe = jnp.split(proj, [o1, o2, o3, o4, o5], axis=-1)
    y_a = chunk_spatial_gating(a_u, a_v, lp['sp_w'], lp['sp_b'])
    y_b, s_re, s_im = s5_mixer(b_u, lp['ssm_lam_re'], lp['ssm_lam_im'], lp['ssm_b_re'], lp['ssm_b_im'],
                               lp['ssm_c_re'], lp['ssm_c_im'], lp['ssm_d'], lp['ssm_log_step'],
                               lp['glu_w'], lp['glu_b'], x0_re, x0_im)
    y_c, ckv, kpe = mla(c_q, c_kv, k_pe, pos0, lp, past)
    y = jnp.concatenate([y_a, y_b, y_c], axis=-1) @ lp['w_out']
    return y, (ckv, kpe, s_re, s_im, a_v)


def hier_moe(h, rg_w, rg_b, re_w, re_b, w_gate, w_up, w_down):
    T, D = h.shape
    g_logits = (h @ rg_w + rg_b).astype(jnp.float32)
    g_prob = jax.nn.softmax(g_logits, axis=-1)
    g_idx = jnp.argmax(g_logits, axis=-1)
    e_logits = (h @ re_w + re_b).astype(jnp.float32).reshape(T, N_GROUPS, EXPERTS_PER_GROUP)
    e_sel = jnp.take_along_axis(e_logits, g_idx[:, None, None], axis=1)[:, 0]
    e_prob = jax.nn.softmax(e_sel, axis=-1)
    top_p, top_i = lax.top_k(e_prob, TOP_K_IN_GROUP)
    top_p = top_p / jnp.sum(top_p, axis=-1, keepdims=True)
    weight = jnp.take_along_axis(g_prob, g_idx[:, None], axis=1) * top_p
    expert = g_idx[:, None] * EXPERTS_PER_GROUP + top_i
    A = T * TOP_K_IN_GROUP
    eid = expert.reshape(A)
    tok = jnp.repeat(jnp.arange(T), TOP_K_IN_GROUP)
    wgt = weight.reshape(A)
    order = jnp.argsort(eid)
    eid_s, tok_s, wgt_s = eid[order], tok[order], wgt[order]
    counts = jnp.bincount(eid, length=N_EXPERTS)
    start = jnp.cumsum(counts) - counts
    padded = ((counts + MOE_BLOCK - 1) // MOE_BLOCK) * MOE_BLOCK
    pend = jnp.cumsum(padded)
    pstart = pend - padded
    dest = pstart[eid_s] + jnp.arange(A) - start[eid_s]
    n_blocks = (A + MOE_BLOCK - 1) // MOE_BLOCK + N_EXPERTS
    P = n_blocks * MOE_BLOCK
    xbuf = jnp.zeros((P, D), h.dtype).at[dest].set(h[tok_s])
    block_e = jnp.clip(jnp.searchsorted(pend, jnp.arange(n_blocks) * MOE_BLOCK, side='right'), 0, N_EXPERTS - 1)

    def run_block(args):
        xb, e = args
        return (jax.nn.silu(xb @ w_gate[e]) * (xb @ w_up[e])) @ w_down[e]

    ybuf = lax.map(run_block, (xbuf.reshape(n_blocks, MOE_BLOCK, D), block_e)).reshape(P, D)
    y_s = ybuf[dest] * wgt_s[:, None].astype(h.dtype)
    return jax.ops.segment_sum(y_s, tok_s, num_segments=T)


def decoder_layer(x, c, pos0, lp, x0_re, x0_im, past):
    B, S, D = x.shape
    mod = jax.nn.silu(c) @ lp['ada_w'] + lp['ada_b']
    sh1, sc1, g1, sh2, sc2, g2 = [m[:, None, :] for m in jnp.split(mod, 6, axis=-1)]
    h = rmsnorm(x, lp['norm1_g']) * (1 + sc1) + sh1
    mix, states = token_mixers(h, pos0, lp, x0_re, x0_im, past)
    x = x + g1 * mix
    h = rmsnorm(x, lp['norm2_g']) * (1 + sc2) + sh2
    ff = hier_moe(h.reshape(B * S, D), lp['router_g_w'], lp['router_g_b'], lp['router_e_w'],
                  lp['router_e_b'], lp['exp_w_gate'], lp['exp_w_up'], lp['exp_w_down'])
    x = x + g2 * ff.reshape(B, S, D)
    return x, states


def setup_inputs(seed: int = 0) -> dict:
    key = jax.random.key(seed)
    ks = iter(jax.random.split(key, 64))

    def nrm(shape, s):
        return jax.random.normal(next(ks), shape, jnp.float32) * s

    def gain(shape):
        return 1.0 + nrm(shape, 0.01)

    n_pages = PAST_LEN // PAGE_SIZE
    n_used = DEC_BATCH * n_pages
    n_phys = n_used + n_used // 4
    page_table = jax.random.permutation(next(ks), n_phys)[:n_used].reshape(DEC_BATCH, n_pages).astype(jnp.int32)
    lam_im = jnp.broadcast_to(math.pi * jnp.arange(SSM_STATE, dtype=jnp.float32), (DEPTH, SSM_GROUPS, SSM_STATE))
    return {
        'x_prompt': nrm((BATCH, SEQ, D_MODEL), 1.0),
        'x_sample': nrm((DEC_BATCH, DEC_SEQ, D_MODEL), 1.0),
        'cache_ckv': nrm((DEPTH, n_phys, PAGE_SIZE, KV_RANK), 1.0),
        'cache_kpe': nrm((DEPTH, n_phys, PAGE_SIZE, QK_ROPE), 1.0),
        'state_ssm_re': nrm((DEPTH, DEC_BATCH, SSM_GROUPS, SSM_STATE), 0.3),
        'state_ssm_im': nrm((DEPTH, DEC_BATCH, SSM_GROUPS, SSM_STATE), 0.3),
        'page_table': page_table,
        'c_prompt': nrm((BATCH, D_MODEL), 1.0),
        'c_sample': nrm((DEC_BATCH, D_MODEL), 1.0),
        'norm1_g': gain((DEPTH, D_MODEL)),
        'norm2_g': gain((DEPTH, D_MODEL)),
        'ada_w': nrm((DEPTH, D_MODEL, 6 * D_MODEL), 0.5 * D_MODEL ** -0.5),
        'ada_b': nrm((DEPTH, 6 * D_MODEL), 0.02),
        'w_in': nrm((DEPTH, D_MODEL, IN_WIDTH), D_MODEL ** -0.5),
        'sp_w': nrm((DEPTH, A_HEADS, CHUNK, CHUNK), CHUNK ** -0.5),
        'sp_b': 1.0 + nrm((DEPTH, A_HEADS, CHUNK), 0.1),
        'ssm_lam_re': -0.5 + nrm((DEPTH, SSM_GROUPS, SSM_STATE), 0.01),
        'ssm_lam_im': lam_im + nrm((DEPTH, SSM_GROUPS, SSM_STATE), 0.01),
        'ssm_b_re': nrm((DEPTH, SSM_GROUPS, SSM_STATE, SSM_CH), (2.0 * SSM_CH) ** -0.5),
        'ssm_b_im': nrm((DEPTH, SSM_GROUPS, SSM_STATE, SSM_CH), (2.0 * SSM_CH) ** -0.5),
        'ssm_c_re': nrm((DEPTH, SSM_GROUPS, SSM_CH, SSM_STATE), (2.0 * SSM_STATE) ** -0.5),
        'ssm_c_im': nrm((DEPTH, SSM_GROUPS, SSM_CH, SSM_STATE), (2.0 * SSM_STATE) ** -0.5),
        'ssm_d': nrm((DEPTH, SSM_WIDTH), 1.0),
        'ssm_log_step': jax.random.uniform(next(ks), (DEPTH, SSM_GROUPS), jnp.float32, math.log(0.001), math.log(0.1)),
        'glu_w': nrm((DEPTH, SSM_WIDTH, SSM_WIDTH), SSM_WIDTH ** -0.5),
        'glu_b': nrm((DEPTH, SSM_WIDTH), 0.02),
        'q_norm_g': gain((DEPTH, Q_RANK)),
        'w_uq': nrm((DEPTH, Q_RANK, MLA_HEADS * (QK_NOPE + QK_ROPE)), Q_RANK ** -0.5),
        'kv_norm_g': gain((DEPTH, KV_RANK)),
        'w_uk': nrm((DEPTH, KV_RANK, MLA_HEADS, QK_NOPE), KV_RANK ** -0.5),
        'w_uv': nrm((DEPTH, KV_RANK, MLA_HEADS, V_DIM), KV_RANK ** -0.5),
        'w_out': nrm((DEPTH, D_MIX, D_MODEL), D_MIX ** -0.5),
        'router_g_w': nrm((DEPTH, D_MODEL, N_GROUPS), D_MODEL ** -0.5),
        'router_g_b': nrm((DEPTH, N_GROUPS), 0.01),
        'router_e_w': nrm((DEPTH, D_MODEL, N_EXPERTS), D_MODEL ** -0.5),
        'router_e_b': nrm((DEPTH, N_EXPERTS), 0.01),
        'exp_w_gate': nrm((DEPTH, N_EXPERTS, D_MODEL, D_EXPERT), D_MODEL ** -0.5),
        'exp_w_up': nrm((DEPTH, N_EXPERTS, D_MODEL, D_EXPERT), D_MODEL ** -0.5),
        'exp_w_down': nrm((DEPTH, N_EXPERTS, D_EXPERT, D_MODEL), D_EXPERT ** -0.5),
        'final_norm_g': gain((D_MODEL,)),
    }


def reference(x_prompt, x_sample, cache_ckv, cache_kpe, state_ssm_re, state_ssm_im, page_table,
              c_prompt, c_sample, norm1_g, norm2_g, ada_w, ada_b, w_in, sp_w, sp_b,
              ssm_lam_re, ssm_lam_im, ssm_b_re, ssm_b_im, ssm_c_re, ssm_c_im, ssm_d, ssm_log_step,
              glu_w, glu_b, q_norm_g, w_uq, kv_norm_g, w_uk, w_uv, w_out,
              router_g_w, router_g_b, router_e_w, router_e_b, exp_w_gate, exp_w_up, exp_w_down,
              final_norm_g):
    past_len = page_table.shape[1] * cache_ckv.shape[2]
    n_dec = x_sample.shape[0]
    n_pr = x_prompt.shape[0]
    xp, xs = x_prompt, x_sample
    p_ckv, p_kpe, p_sre, p_sim = [], [], [], []
    s_ckv, s_kpe, s_sre, s_sim, s_v = [], [], [], [], []
    for l in range(DEPTH):
        lp = {
            'norm1_g': norm1_g[l], 'norm2_g': norm2_g[l], 'ada_w': ada_w[l], 'ada_b': ada_b[l],
            'w_in': w_in[l], 'sp_w': sp_w[l], 'sp_b': sp_b[l],
            'ssm_lam_re': ssm_lam_re[l], 'ssm_lam_im': ssm_lam_im[l], 'ssm_b_re': ssm_b_re[l],
            'ssm_b_im': ssm_b_im[l], 'ssm_c_re': ssm_c_re[l], 'ssm_c_im': ssm_c_im[l], 'ssm_d': ssm_d[l],
            'ssm_log_step': ssm_log_step[l], 'glu_w': glu_w[l], 'glu_b': glu_b[l],
            'q_norm_g': q_norm_g[l], 'w_uq': w_uq[l], 'kv_norm_g': kv_norm_g[l], 'w_uk': w_uk[l],
            'w_uv': w_uv[l], 'w_out': w_out[l],
            'router_g_w': router_g_w[l], 'router_g_b': router_g_b[l], 'router_e_w': router_e_w[l],
            'router_e_b': router_e_b[l], 'exp_w_gate': exp_w_gate[l], 'exp_w_up': exp_w_up[l],
            'exp_w_down': exp_w_down[l],
        }
        zeros = jnp.zeros((n_pr, SSM_GROUPS, SSM_STATE), xp.dtype)
        xp, (ck, kp, sr, si, _) = decoder_layer(xp, c_prompt, 0, lp, zeros, zeros, None)
        p_ckv.append(ck)
        p_kpe.append(kp)
        p_sre.append(sr)
        p_sim.append(si)
        ckv_past = cache_ckv[l][page_table].reshape(n_dec, past_len, KV_RANK)
        kpe_past = cache_kpe[l][page_table].reshape(n_dec, past_len, QK_ROPE)
        xs, (ck, kp, sr, si, va) = decoder_layer(xs, c_sample, past_len, lp, state_ssm_re[l],
                                                 state_ssm_im[l], (ckv_past, kpe_past))
        s_ckv.append(ck)
        s_kpe.append(kp)
        s_sre.append(sr)
        s_sim.append(si)
        s_v.append(va)
    y_prompt = rmsnorm(xp, final_norm_g)
    y_sample = rmsnorm(xs, final_norm_g)
    return (y_prompt, y_sample,
            jnp.stack(p_ckv), jnp.stack(p_kpe), jnp.stack(p_sre), jnp.stack(p_sim),
            jnp.stack(s_ckv), jnp.stack(s_kpe), jnp.stack(s_sre), jnp.stack(s_sim), jnp.stack(s_v))
```

```python
import functools
import math

import jax
import jax.numpy as jnp
from jax import lax
from jax.experimental import pallas as pl
from jax.experimental.pallas import tpu as pltpu

F32 = jnp.float32
BF16 = jnp.bfloat16

D_MODEL = 1024
RMS_EPS = 1e-6
A_HEADS = 4
A_HEAD_DIM = 64
A_WIDTH = 256
CHUNK = 128
SSM_CH = 16
SSM_WIDTH = 256
SSM_GROUPS = 16
SSM_STATE = 64
SSM_FLAT = SSM_GROUPS * SSM_STATE
MLA_HEADS = 8
QK_NOPE = 64
QK_ROPE = 32
V_DIM = 64
Q_RANK = 256
KV_RANK = 128
MLA_WIDTH = 512
ROPE_THETA = 10000.0
ATTN_SCALE = (QK_NOPE + QK_ROPE) ** -0.5
N_GROUPS = 4
EXPERTS_PER_GROUP = 8
N_EXPERTS = 32
D_EXPERT = 512
KCAT = 256
IN_PAD = 1408

LANES = 128
SUBLANES = 8
VMEM_LIMIT = 56 * 1024 * 1024

NEG = -1e30


def _cparams(*sem):
    return pltpu.CompilerParams(dimension_semantics=tuple(sem), vmem_limit_bytes=VMEM_LIMIT)


def _dot(a, b):
    return jnp.dot(a, b, preferred_element_type=F32)


def _dot_nt(a, b):
    return lax.dot_general(a, b, (((1,), (1,)), ((), ())), preferred_element_type=F32)


def _rms(x):
    return x * lax.rsqrt(jnp.mean(x * x, axis=-1, keepdims=True) + RMS_EPS)


def _mod_kernel(c_ref, w_ref, b_ref, o_ref):
    c = c_ref[...]
    a = (c * jax.nn.sigmoid(c)).astype(BF16)
    o_ref[...] = _dot(a, w_ref[...].astype(BF16)) + b_ref[...]


def _modulation(c_all, ada_w, ada_b):
    depth = ada_w.shape[0]
    nb = c_all.shape[0]
    n_out = ada_w.shape[2]
    tn = D_MODEL
    return pl.pallas_call(
        _mod_kernel,
        grid=(depth, n_out // tn),
        in_specs=[
            pl.BlockSpec((nb, D_MODEL), lambda l, n: (0, 0)),
            pl.BlockSpec((None, D_MODEL, tn), lambda l, n: (l, 0, n)),
            pl.BlockSpec((None, 1, tn), lambda l, n: (l, 0, n)),
        ],
        out_specs=pl.BlockSpec((None, nb, tn), lambda l, n: (l, 0, n)),
        out_shape=jax.ShapeDtypeStruct((depth, nb, n_out), F32),
        compiler_params=_cparams("arbitrary", "arbitrary"),
        name="ada_mod",
    )(c_all, ada_w, ada_b.reshape(depth, 1, n_out))


def _proj_kernel(x_ref, sh_ref, sc_ref, g1_ref, win_ref, gma_ref, gmb_ref, qg_ref, wq_ref, wuk_ref, kvg_ref,
                 cq_ref, sq_ref, ck_ref, sk_ref,
                 ya_ref, bu_ref, qlat_ref, qpe_ref, ckv_ref, kpe_ref, kcat_ref, av_ref, *, chunked, tm):
    x = x_ref[...]
    h = _rms(x) * g1_ref[...] * (1.0 + sc_ref[...]) + sh_ref[...]
    proj = _dot(h.astype(BF16), win_ref[...])
    a_u = proj[:, 0:256]
    a_v = proj[:, 256:512]
    bu_ref[...] = proj[:, 512:768]
    c_q = proj[:, 768:1024]
    c_kv = proj[:, 1024:1152]
    kp4 = proj[:, 1152:1280]
    kp4s = proj[:, 1280:1408]
    av_ref[...] = a_v

    if chunked:
        lane_head = lax.broadcasted_iota(jnp.int32, (CHUNK, A_WIDTH), 1) // A_HEAD_DIM
        for c in range(tm // CHUNK):
            vb = a_v[c * CHUNK:(c + 1) * CHUNK].astype(BF16)
            vstack = jnp.concatenate(
                [jnp.where(lane_head == hd, vb, jnp.zeros_like(vb)) for hd in range(A_HEADS)], axis=0)
            gate = _dot(gma_ref[...], vstack) + gmb_ref[...]
            ya_ref[c * CHUNK:(c + 1) * CHUNK, :] = (a_u[c * CHUNK:(c + 1) * CHUNK] * gate).astype(BF16)
    else:
        ya_ref[...] = (a_u * (gma_ref[...] * a_v + gmb_ref[...])).astype(BF16)

    cq = (_rms(c_q) * qg_ref[...]).astype(BF16)
    qall = _dot(cq, wq_ref[...])
    qpe_ref[...] = (qall[:, 512:768] * cq_ref[...] + qall[:, 768:1024] * sq_ref[...]).astype(BF16)
    for j in range(MLA_HEADS // 2):
        qn = qall[:, j * 128:(j + 1) * 128].astype(BF16)
        qlat_ref[:, j * 256:(j + 1) * 256] = _dot(qn, wuk_ref[j]).astype(BF16)
    ckv = _rms(c_kv) * kvg_ref[...]
    ckv_ref[...] = ckv
    kpe4 = kp4 * ck_ref[...] + kp4s * sk_ref[...]
    kpe_ref[...] = kpe4[:, 0:QK_ROPE]
    kcat_ref[:, 0:KV_RANK] = ckv.astype(BF16)
    kcat_ref[:, KV_RANK:KCAT] = kpe4.astype(BF16)


def _proj(x, mod3, lw, rope, *, seq, tm, chunked):
    T = x.shape[0]
    nt = T // tm
    if chunked:
        per = seq // tm

        def modspec(k):
            return pl.BlockSpec((None, 1, D_MODEL), lambda i: (i // per, 0, k))

        def ropespec(w):
            return pl.BlockSpec((tm, w), lambda i: (i % per, 0))
        gma, gmb = lw["gm_wcat"], lw["gm_bt"]
    else:
        def modspec(k):
            return pl.BlockSpec((tm, D_MODEL), lambda i: (i, k))

        def ropespec(w):
            return pl.BlockSpec((tm, w), lambda i: (i, 0))
        gma, gmb = lw["gm_w0"], lw["gm_b0"]

    def full(a):
        nd = a.ndim
        return pl.BlockSpec(a.shape, lambda i: (0,) * nd)

    def rows(w):
        return pl.BlockSpec((tm, w), lambda i: (i, 0))

    outs = [
        jax.ShapeDtypeStruct((T, A_WIDTH), BF16),
        jax.ShapeDtypeStruct((T, SSM_WIDTH), F32),
        jax.ShapeDtypeStruct((T, MLA_HEADS * KV_RANK), BF16),
        jax.ShapeDtypeStruct((T, MLA_HEADS * QK_ROPE), BF16),
        jax.ShapeDtypeStruct((T, KV_RANK), F32),
        jax.ShapeDtypeStruct((T, QK_ROPE), F32),
        jax.ShapeDtypeStruct((T, KCAT), BF16),
        jax.ShapeDtypeStruct((T, A_WIDTH), F32),
    ]
    return pl.pallas_call(
        functools.partial(_proj_kernel, chunked=chunked, tm=tm),
        grid=(nt,),
        in_specs=[rows(D_MODEL), modspec(0), modspec(1), full(lw["norm1_g"]), full(lw["w_in"]), full(gma), full(gmb),
                  full(lw["q_norm_g"]), full(lw["wq"]), full(lw["wuk_bd"]), full(lw["kv_norm_g"]),
                  ropespec(256), ropespec(256), ropespec(128), ropespec(128)],
        out_specs=[rows(o.shape[1]) for o in outs],
        out_shape=outs,
        compiler_params=_cparams("arbitrary"),
        name="proj",
    )(x, mod3, mod3, lw["norm1_g"], lw["w_in"], gma, gmb, lw["q_norm_g"], lw["wq"], lw["wuk_bd"], lw["kv_norm_g"],
      rope["cq"], rope["sq"], rope["ck"], rope["sk"])


def _gelu_glu(y, gw_ref, gb_ref):
    z = jax.nn.gelu(y)
    gate = jax.nn.sigmoid(_dot(z.astype(BF16), gw_ref[...]) + gb_ref[...])
    return z * gate


S5_SUB = 32
S5_SEQS = 8


def _s5_prompt_kernel(u_ref, p_ref, pt_ref, bbd_ref, ar_ref, ai_ref, cbd_ref, d_ref, gw_ref, gb_ref,
                      yb_ref, sre_ref, sim_ref, bu_s, xs_s, st_s, *, nsub):
    i = pl.program_id(1)

    @pl.when(i == 0)
    def _():
        st_s[...] = jnp.zeros_like(st_s)

    rows = S5_SEQS * S5_SUB
    ar = jnp.broadcast_to(ar_ref[...], (S5_SEQS, SSM_FLAT))
    ai = jnp.broadcast_to(ai_ref[...], (S5_SEQS, SSM_FLAT))
    for k in range(nsub):
        u = u_ref[:, k * S5_SUB:(k + 1) * S5_SUB, :].reshape(rows, SSM_WIDTH)
        hi = u.astype(BF16)
        r1 = u - hi.astype(F32)
        mid = r1.astype(BF16)
        lo = (r1 - mid.astype(F32)).astype(BF16)
        up_hi = _dot(p_ref[...], hi)
        up = up_hi + _dot(p_ref[...], mid) + _dot(p_ref[...], lo)
        bu_s[...] = _dot(up_hi.astype(BF16), bbd_ref[...])

        def step(t, x):
            r = pl.multiple_of(t * S5_SEQS, S5_SEQS)
            b = bu_s[pl.ds(r, S5_SEQS), :]
            xr = x[:, :SSM_FLAT]
            xi = x[:, SSM_FLAT:]
            nr = ar * xr - ai * xi + b[:, :SSM_FLAT]
            ni = ar * xi + ai * xr + b[:, SSM_FLAT:]
            xn = jnp.concatenate([nr, ni], axis=-1)
            xs_s[pl.ds(r, S5_SEQS), :] = xn
            return xn

        x = lax.fori_loop(0, S5_SUB, step, st_s[...], unroll=4)
        st_s[...] = x
        y = _dot(xs_s[...].astype(BF16), cbd_ref[...]) + d_ref[...] * up
        yb = _gelu_glu(y, gw_ref, gb_ref).astype(BF16)
        back = _dot(pt_ref[...], yb).astype(BF16)
        yb_ref[:, k * S5_SUB:(k + 1) * S5_SUB, :] = back.reshape(S5_SEQS, S5_SUB, SSM_WIDTH)

    @pl.when(i == pl.num_programs(1) - 1)
    def _():
        sre_ref[...] = st_s[:, :SSM_FLAT]
        sim_ref[...] = st_s[:, SSM_FLAT:]


def _s5_prompt(bu, lw, consts, *, batch, seq):
    tt = min(128, seq)
    nsub = tt // S5_SUB
    u3 = bu.reshape(batch, seq, SSM_WIDTH)

    def full(a):
        nd = a.ndim
        return pl.BlockSpec(a.shape, lambda j, i: (0,) * nd)

    rows = S5_SEQS * S5_SUB
    yb, sre, sim = pl.pallas_call(
        functools.partial(_s5_prompt_kernel, nsub=nsub),
        grid=(batch // S5_SEQS, seq // tt),
        in_specs=[pl.BlockSpec((S5_SEQS, tt, SSM_WIDTH), lambda j, i: (j, i, 0)),
                  full(consts["perm"]), full(consts["perm_t"]), full(lw["bbd"]), full(lw["a_re"]), full(lw["a_im"]),
                  full(lw["cbd"]), full(lw["ssm_d"]), full(lw["glu_w"]), full(lw["glu_b"])],
        out_specs=[pl.BlockSpec((S5_SEQS, tt, SSM_WIDTH), lambda j, i: (j, i, 0)),
                   pl.BlockSpec((S5_SEQS, SSM_FLAT), lambda j, i: (j, 0)),
                   pl.BlockSpec((S5_SEQS, SSM_FLAT), lambda j, i: (j, 0))],
        out_shape=[jax.ShapeDtypeStruct((batch, seq, SSM_WIDTH), BF16),
                   jax.ShapeDtypeStruct((batch, SSM_FLAT), F32),
                   jax.ShapeDtypeStruct((batch, SSM_FLAT), F32)],
        scratch_shapes=[pltpu.VMEM((rows, 2 * SSM_FLAT), F32), pltpu.VMEM((rows, 2 * SSM_FLAT), F32),
                        pltpu.VMEM((S5_SEQS, 2 * SSM_FLAT), F32)],
        compiler_params=_cparams("arbitrary", "arbitrary"),
        name="s5_prompt",
    )(u3, consts["perm"], consts["perm_t"], lw["bbd"], lw["a_re"], lw["a_im"], lw["cbd"], lw["ssm_d"],
      lw["glu_w"], lw["glu_b"])
    return yb.reshape(batch * seq, SSM_WIDTH), sre, sim


def _s5_step_kernel(u_ref, x0r_ref, x0i_ref, bbd_ref, ar_ref, ai_ref, cbd_ref, d_ref, gw_ref, gb_ref,
                    yb_ref, sre_ref, sim_ref):
    u = u_ref[...]
    bu = _dot(u.astype(BF16), bbd_ref[...])
    ar = ar_ref[...]
    ai = ai_ref[...]
    xr = x0r_ref[...]
    xi = x0i_ref[...]
    nr = ar * xr - ai * xi + bu[:, :SSM_FLAT]
    ni = ar * xi + ai * xr + bu[:, SSM_FLAT:]
    sre_ref[...] = nr
    sim_ref[...] = ni
    xcat = jnp.concatenate([nr, ni], axis=-1).astype(BF16)
    y = _dot(xcat, cbd_ref[...]) + d_ref[...] * u
    yb_ref[...] = _gelu_glu(y, gw_ref, gb_ref).astype(BF16)


def _s5_step(bu, x0r, x0i, lw):
    n = bu.shape[0]
    args = (bu, x0r, x0i, lw["bbd"], lw["a_re"], lw["a_im"], lw["cbd"], lw["ssm_d"], lw["glu_w"], lw["glu_b"])
    return pl.pallas_call(
        _s5_step_kernel,
        out_shape=[jax.ShapeDtypeStruct((n, SSM_WIDTH), BF16), jax.ShapeDtypeStruct((n, SSM_FLAT), F32),
                   jax.ShapeDtypeStruct((n, SSM_FLAT), F32)],
        compiler_params=pltpu.CompilerParams(vmem_limit_bytes=VMEM_LIMIT),
        name="s5_step",
    )(*args)


def _attn_prompt_kernel(qlat_ref, qpe_ref, k_ref, wuv_ref, yc_ref, qs_s, m_s, l_s, acc_s, *, tq, tk):
    i = pl.program_id(1)
    lane_head = lax.broadcasted_iota(jnp.int32, (tq, LANES), 1) // QK_ROPE
    for hd in range(MLA_HEADS):
        qs_s[hd * tq:(hd + 1) * tq, 0:KV_RANK] = qlat_ref[:, hd * KV_RANK:(hd + 1) * KV_RANK]
        grp = qpe_ref[:, (hd // 4) * LANES:(hd // 4 + 1) * LANES]
        qs_s[hd * tq:(hd + 1) * tq, KV_RANK:KCAT] = jnp.where(lane_head == hd % 4, grp, jnp.zeros_like(grp))
    nr = MLA_HEADS * tq
    m_s[...] = jnp.full((nr, 1), NEG, F32)
    l_s[...] = jnp.zeros((nr, 1), F32)
    acc_s[...] = jnp.zeros((nr, KV_RANK), F32)
    qpos = i * tq + lax.broadcasted_iota(jnp.int32, (nr, tk), 0) % tq
    kcol = lax.broadcasted_iota(jnp.int32, (nr, tk), 1)

    def body(j, carry):
        k0 = pl.multiple_of(j * tk, tk)
        kt = k_ref[pl.ds(k0, tk), :]
        s = _dot_nt(qs_s[...], kt) * ATTN_SCALE
        s = jnp.where(kcol + j * tk <= qpos, s, NEG)
        m_old = m_s[...]
        m_new = jnp.maximum(m_old, jnp.max(s, axis=-1, keepdims=True))
        alpha = jnp.exp(m_old - m_new)
        p = jnp.exp(s - m_new)
        l_s[...] = alpha * l_s[...] + jnp.sum(p, axis=-1, keepdims=True)
        acc_s[...] = alpha * acc_s[...] + _dot(p.astype(BF16), kt[:, 0:KV_RANK])
        m_s[...] = m_new
        return carry

    lax.fori_loop(0, (i * tq) // tk + (tq + tk - 1) // tk, body, 0)
    o = acc_s[...] / l_s[...]
    for j in range(MLA_HEADS // 2):
        pair = jnp.concatenate([o[(2 * j) * tq:(2 * j + 1) * tq], o[(2 * j + 1) * tq:(2 * j + 2) * tq]], axis=-1)
        yc_ref[:, j * 128:(j + 1) * 128] = _dot(pair.astype(BF16), wuv_ref[j]).astype(BF16)


def _attn_prompt(qlat, qpe, kcat, lw, *, batch, seq):
    tq = min(256, seq)
    tk = tq
    nr = MLA_HEADS * tq
    yc = pl.pallas_call(
        functools.partial(_attn_prompt_kernel, tq=tq, tk=tk),
        grid=(batch, seq // tq),
        in_specs=[pl.BlockSpec((None, tq, MLA_HEADS * KV_RANK), lambda b, i: (b, i, 0)),
                  pl.BlockSpec((None, tq, MLA_HEADS * QK_ROPE), lambda b, i: (b, i, 0)),
                  pl.BlockSpec((None, seq, KCAT), lambda b, i: (b, 0, 0)),
                  pl.BlockSpec(lw["wuv_bd"].shape, lambda b, i: (0, 0, 0))],
        out_specs=pl.BlockSpec((None, tq, MLA_WIDTH), lambda b, i: (b, i, 0)),
        out_shape=jax.ShapeDtypeStruct((batch, seq, MLA_WIDTH), BF16),
        scratch_shapes=[pltpu.VMEM((nr, KCAT), BF16), pltpu.VMEM((nr, 1), F32), pltpu.VMEM((nr, 1), F32),
                        pltpu.VMEM((nr, KV_RANK), F32)],
        compiler_params=_cparams("arbitrary", "arbitrary"),
        name="attn_prompt",
    )(qlat.reshape(batch, seq, -1), qpe.reshape(batch, seq, -1), kcat.reshape(batch, seq, KCAT), lw["wuv_bd"])
    return yc.reshape(batch * seq, MLA_WIDTH)


def _attn_sample_kernel(pt_ref, ql_ref, qp_ref, kn_ref, pn_ref, wuv_ref, ckv_hbm, kpe_hbm, yc_ref,
                        cbuf, pbuf, sem, m_s, l_s, acc_s, *, layer, pc, nchunk, page):
    b = pl.program_id(0)
    c = pl.program_id(1)
    step = b * nchunk + c
    nstep = pl.num_programs(0) * nchunk
    slot = step % 2

    def copies(bb, cc, sl):
        out = []
        for p in range(pc):
            pg = pt_ref[bb, cc * pc + p]
            out.append(pltpu.make_async_copy(ckv_hbm.at[layer, pg], cbuf.at[sl, p], sem.at[sl, 0]))
            out.append(pltpu.make_async_copy(kpe_hbm.at[layer, pg], pbuf.at[sl, p], sem.at[sl, 1]))
        return out

    @pl.when(step == 0)
    def _():
        for cp in copies(b, c, slot):
            cp.start()

    @pl.when(step + 1 < nstep)
    def _():
        nxt = step + 1
        for cp in copies(nxt // nchunk, nxt % nchunk, 1 - slot):
            cp.start()

    ql = ql_ref[...]
    qp = qp_ref[...]

    @pl.when(c == 0)
    def _():
        kn = kn_ref[...].astype(BF16).astype(F32)
        pn = pn_ref[...].astype(BF16).astype(F32)
        s_new = (jnp.sum(ql.astype(F32) * kn, axis=-1, keepdims=True)
                 + jnp.sum(qp.astype(F32) * pn, axis=-1, keepdims=True)) * ATTN_SCALE
        m_s[...] = s_new
        l_s[...] = jnp.ones_like(l_s)
        acc_s[...] = jnp.broadcast_to(kn, acc_s.shape)

    for cp in copies(b, c, slot):
        cp.wait()

    kc = cbuf[slot].reshape(pc * page, KV_RANK).astype(BF16)
    kp = pbuf[slot].reshape(pc * page, QK_ROPE).astype(BF16)
    s = (_dot_nt(ql, kc) + _dot_nt(qp, kp)) * ATTN_SCALE
    m_old = m_s[...]
    m_new = jnp.maximum(m_old, jnp.max(s, axis=-1, keepdims=True))
    alpha = jnp.exp(m_old - m_new)
    p = jnp.exp(s - m_new)
    l_s[...] = alpha * l_s[...] + jnp.sum(p, axis=-1, keepdims=True)
    acc_s[...] = alpha * acc_s[...] + _dot(p.astype(BF16), kc)
    m_s[...] = m_new

    @pl.when(c == nchunk - 1)
    def _():
        o = (acc_s[...] / l_s[...]).astype(BF16)
        yfull = _dot(o, wuv_ref[...])
        sel = lax.broadcasted_iota(jnp.int32, yfull.shape, 1) // V_DIM == lax.broadcasted_iota(
            jnp.int32, yfull.shape, 0)
        yc_ref[...] = jnp.sum(jnp.where(sel, yfull, 0.0), axis=0, keepdims=True).astype(BF16)


def _attn_sample(qlat, qpe, ckv_new, kpe_new, page_table, cache_ckv, cache_kpe, lw, *, layer):
    n, npages = page_table.shape
    page = cache_ckv.shape[2]
    pc = 16
    while npages % pc:
        pc //= 2
    nchunk = npages // pc
    ql = qlat.reshape(n, MLA_HEADS, KV_RANK)
    qp = qpe.reshape(n, MLA_HEADS, QK_ROPE)
    grid_spec = pltpu.PrefetchScalarGridSpec(
        num_scalar_prefetch=1,
        grid=(n, nchunk),
        in_specs=[pl.BlockSpec((None, MLA_HEADS, KV_RANK), lambda b, c, pt: (b, 0, 0)),
                  pl.BlockSpec((None, MLA_HEADS, QK_ROPE), lambda b, c, pt: (b, 0, 0)),
                  pl.BlockSpec((None, 1, KV_RANK), lambda b, c, pt: (b, 0, 0)),
                  pl.BlockSpec((None, 1, QK_ROPE), lambda b, c, pt: (b, 0, 0)),
                  pl.BlockSpec(lw["wuv_flat"].shape, lambda b, c, pt: (0, 0)),
                  pl.BlockSpec(memory_space=pl.ANY),
                  pl.BlockSpec(memory_space=pl.ANY)],
        out_specs=pl.BlockSpec((None, 1, MLA_WIDTH), lambda b, c, pt: (b, 0, 0)),
        scratch_shapes=[pltpu.VMEM((2, pc, page, KV_RANK), F32), pltpu.VMEM((2, pc, page, QK_ROPE), F32),
                        pltpu.SemaphoreType.DMA((2, 2)),
                        pltpu.VMEM((MLA_HEADS, 1), F32), pltpu.VMEM((MLA_HEADS, 1), F32),
                        pltpu.VMEM((MLA_HEADS, KV_RANK), F32)],
    )
    yc = pl.pallas_call(
        functools.partial(_attn_sample_kernel, layer=layer, pc=pc, nchunk=nchunk, page=page),
        grid_spec=grid_spec,
        out_shape=jax.ShapeDtypeStruct((n, 1, MLA_WIDTH), BF16),
        compiler_params=_cparams("arbitrary", "arbitrary"),
        name="attn_sample",
    )(page_table, ql, qp, ckv_new.reshape(n, 1, KV_RANK), kpe_new.reshape(n, 1, QK_ROPE), lw["wuv_flat"],
      cache_ckv, cache_kpe)
    return yc.reshape(n, MLA_WIDTH)


def _post_kernel(x_ref, ya_ref, yb_ref, yc_ref, wo_ref, g1_ref, n2_ref, sc2_ref, sh2_ref, wrh_ref, wrl_ref, br_ref,
                 ltri_ref, x1_ref, h2_ref, route_ref, cnt_ref, run_s):
    i = pl.program_id(0)

    @pl.when(i == 0)
    def _():
        run_s[...] = jnp.zeros_like(run_s)

    y = (_dot(ya_ref[...], wo_ref[0:256, :]) + _dot(yb_ref[...], wo_ref[256:512, :])
         + _dot(yc_ref[...], wo_ref[512:1024, :]))
    x1 = x_ref[...] + g1_ref[...] * y
    x1_ref[...] = x1
    h2 = _rms(x1) * n2_ref[...] * (1.0 + sc2_ref[...]) + sh2_ref[...]
    h2_ref[...] = h2

    hh = h2.astype(BF16)
    hl = (h2 - hh.astype(F32)).astype(BF16)
    logits = _dot(hh, wrh_ref[...]) + _dot(hh, wrl_ref[...]) + _dot(hl, wrh_ref[...]) + br_ref[...]
    tm = logits.shape[0]
    lane = lax.broadcasted_iota(jnp.int32, (tm, LANES), 1).astype(F32)
    big = float(LANES)

    gl = jnp.where(lane < N_GROUPS, logits, NEG)
    gmax = jnp.max(gl, axis=-1, keepdims=True)
    gidx = jnp.min(jnp.where(gl == gmax, lane, big), axis=-1, keepdims=True)
    gden = jnp.sum(jnp.where(lane < N_GROUPS, jnp.exp(gl - gmax), 0.0), axis=-1, keepdims=True)
    gprob = 1.0 / gden

    lo = N_GROUPS + EXPERTS_PER_GROUP * gidx
    el = jnp.where((lane >= lo) & (lane < lo + EXPERTS_PER_GROUP), logits, NEG)
    m1 = jnp.max(el, axis=-1, keepdims=True)
    i1 = jnp.min(jnp.where(el == m1, lane, big), axis=-1, keepdims=True)
    el2 = jnp.where(lane == i1, NEG, el)
    m2 = jnp.max(el2, axis=-1, keepdims=True)
    i2 = jnp.min(jnp.where(el2 == m2, lane, big), axis=-1, keepdims=True)
    e21 = jnp.exp(m2 - m1)
    w1 = gprob / (1.0 + e21)
    w2 = gprob * e21 / (1.0 + e21)
    e1 = i1 - N_GROUPS
    e2 = i2 - N_GROUPS

    oh1 = lane == e1
    oh2 = lane == e2
    both = jnp.where(oh1, 1.0, 0.0) + jnp.where(oh2, 1.0, 0.0)
    tot = _dot(ltri_ref[...], both.astype(BF16)) + run_s[...]
    r1 = jnp.sum(jnp.where(oh1, tot, 0.0), axis=-1, keepdims=True)
    r2 = jnp.sum(jnp.where(oh2, tot, 0.0), axis=-1, keepdims=True)
    run = run_s[...] + jnp.sum(both, axis=0, keepdims=True)
    run_s[...] = run
    cnt_ref[...] = jnp.broadcast_to(run, cnt_ref.shape)

    route = jnp.where(lane == 0, e1, jnp.where(lane == 1, e2, jnp.where(lane == 2, r1, jnp.where(
        lane == 3, r2, jnp.where(lane == 4, w1, jnp.where(lane == 5, w2, 0.0))))))
    route_ref[...] = route[:, 0:SUBLANES]


def _post(x, ya, yb, yc, mod3, lw, consts, *, seq, tm, per_seq):
    T = x.shape[0]
    nt = T // tm
    if per_seq:
        per = seq // tm

        def modspec(k):
            return pl.BlockSpec((None, 1, D_MODEL), lambda i: (i // per, 0, k))
    else:
        def modspec(k):
            return pl.BlockSpec((tm, D_MODEL), lambda i: (i, k))

    def full(a):
        nd = a.ndim
        return pl.BlockSpec(a.shape, lambda i: (0,) * nd)

    def rows(w):
        return pl.BlockSpec((tm, w), lambda i: (i, 0))

    ltri = consts["ltri"][tm]
    return pl.pallas_call(
        _post_kernel,
        grid=(nt,),
        in_specs=[rows(D_MODEL), rows(A_WIDTH), rows(SSM_WIDTH), rows(MLA_WIDTH), full(lw["w_out"]),
                  modspec(2), full(lw["norm2_g"]), modspec(4), modspec(3),
                  full(lw["wr_hi"]), full(lw["wr_lo"]), full(lw["br"]), full(ltri)],
        out_specs=[rows(D_MODEL), rows(D_MODEL), rows(SUBLANES), pl.BlockSpec((SUBLANES, LANES), lambda i: (0, 0))],
        out_shape=[jax.ShapeDtypeStruct((T, D_MODEL), F32), jax.ShapeDtypeStruct((T, D_MODEL), F32),
                   jax.ShapeDtypeStruct((T, SUBLANES), F32), jax.ShapeDtypeStruct((SUBLANES, LANES), F32)],
        scratch_shapes=[pltpu.VMEM((1, LANES), F32)],
        compiler_params=_cparams("arbitrary"),
        name="post",
    )(x, ya, yb, yc, lw["w_out"], mod3, lw["norm2_g"], mod3, mod3, lw["wr_hi"], lw["wr_lo"], lw["br"], ltri)


def _stage_indices(dest_hbm, idx_s, isem):
    i = pl.program_id(0)
    n = pl.num_programs(0)
    slot = i % 2

    def cp(step, sl):
        return pltpu.make_async_copy(dest_hbm.at[step], idx_s.at[sl], isem.at[sl])

    @pl.when(i == 0)
    def _():
        cp(0, 0).start()

    cp(i, slot).wait()

    @pl.when(i + 1 < n)
    def _():
        cp(i + 1, 1 - slot).start()

    return slot


def _dispatch_kernel(dest_hbm, h_ref, xin_hbm, xbuf_hbm, idx_s, isem, rsem, *, tm):
    del xin_hbm
    slot = _stage_indices(dest_hbm, idx_s, isem)

    def row_copy(r, d):
        return pltpu.make_async_copy(h_ref.at[pl.ds(r, 1)], xbuf_hbm.at[pl.ds(d, 1)], rsem.at[0])

    def issue(r, carry):
        row_copy(r, idx_s[slot, r]).start()
        row_copy(r, idx_s[slot, tm + r]).start()
        return carry

    lax.fori_loop(0, tm, issue, 0, unroll=8)

    def drain(r, carry):
        row_copy(0, 0).wait()
        row_copy(0, 0).wait()
        return carry

    lax.fori_loop(0, tm, drain, 0, unroll=8)


def _dispatch(h2, dest2, n_rows, *, tm):
    T = h2.shape[0]
    xzero = jnp.zeros((n_rows, D_MODEL), h2.dtype)
    return pl.pallas_call(
        functools.partial(_dispatch_kernel, tm=tm),
        grid=(T // tm,),
        in_specs=[pl.BlockSpec(memory_space=pl.ANY), pl.BlockSpec((tm, D_MODEL), lambda i: (i, 0)),
                  pl.BlockSpec(memory_space=pl.ANY)],
        out_specs=pl.BlockSpec(memory_space=pl.ANY),
        out_shape=jax.ShapeDtypeStruct((n_rows, D_MODEL), h2.dtype),
        scratch_shapes=[pltpu.SMEM((2, 2 * tm), jnp.int32), pltpu.SemaphoreType.DMA((2,)),
                        pltpu.SemaphoreType.DMA((1,))],
        input_output_aliases={2: 0},
        compiler_params=_cparams("arbitrary"),
        name="moe_dispatch",
    )(dest2, h2, xzero)


def _ffn_kernel(be_ref, nu_ref, x_ref, wg_ref, wu_ref, wd_ref, y_ref):
    i = pl.program_id(0)

    @pl.when(i < nu_ref[0])
    def _():
        x = x_ref[...].astype(BF16)
        g = _dot(x, wg_ref[...])
        u = _dot(x, wu_ref[...])
        mid = (g * jax.nn.sigmoid(g) * u).astype(BF16)
        y_ref[...] = _dot(mid, wd_ref[...])

    @pl.when(i >= nu_ref[0])
    def _():
        y_ref[...] = jnp.zeros_like(y_ref)


def _ffn(xbuf, block_e, n_used, lw, *, blk):
    n_rows = xbuf.shape[0]
    nb = n_rows // blk
    grid_spec = pltpu.PrefetchScalarGridSpec(
        num_scalar_prefetch=2,
        grid=(nb,),
        in_specs=[pl.BlockSpec((blk, D_MODEL), lambda i, be, nu: (i, 0)),
                  pl.BlockSpec((None, D_MODEL, D_EXPERT), lambda i, be, nu: (be[i], 0, 0)),
                  pl.BlockSpec((None, D_MODEL, D_EXPERT), lambda i, be, nu: (be[i], 0, 0)),
                  pl.BlockSpec((None, D_EXPERT, D_MODEL), lambda i, be, nu: (be[i], 0, 0))],
        out_specs=pl.BlockSpec((blk, D_MODEL), lambda i, be, nu: (i, 0)),
    )
    return pl.pallas_call(
        _ffn_kernel,
        grid_spec=grid_spec,
        out_shape=jax.ShapeDtypeStruct((n_rows, D_MODEL), F32),
        compiler_params=_cparams("arbitrary"),
        name="moe_ffn",
    )(block_e, n_used, xbuf, lw["wg"], lw["wu"], lw["wd"])


def _combine_kernel(dest_hbm, x1_ref, g2_ref, route_ref, fg_ref, ybuf_hbm, out_ref, idx_s, isem, ysc, rsem,
                    *, tm, final):
    slot = _stage_indices(dest_hbm, idx_s, isem)

    def row_copy(k, r, d):
        return pltpu.make_async_copy(ybuf_hbm.at[pl.ds(d, 1)], ysc.at[k, pl.ds(r, 1)], rsem.at[0])

    def issue(r, carry):
        row_copy(0, r, idx_s[slot, r]).start()
        row_copy(1, r, idx_s[slot, tm + r]).start()
        return carry

    lax.fori_loop(0, tm, issue, 0, unroll=8)

    def drain(r, carry):
        row_copy(0, 0, 0).wait()
        row_copy(1, 0, 0).wait()
        return carry

    lax.fori_loop(0, tm, drain, 0, unroll=8)

    route = route_ref[...]
    ff = ysc[0] * route[:, 4:5] + ysc[1] * route[:, 5:6]
    x2 = x1_ref[...] + g2_ref[...] * ff
    if final:
        x2 = _rms(x2) * fg_ref[...]
    out_ref[...] = x2


def _combine(x1, mod3, route, ybuf, dest2, final_g, *, seq, tm, per_seq, final):
    T = x1.shape[0]
    if per_seq:
        per = seq // tm
        g2spec = pl.BlockSpec((None, 1, D_MODEL), lambda i: (i // per, 0, 5))
    else:
        g2spec = pl.BlockSpec((tm, D_MODEL), lambda i: (i, 5))
    return pl.pallas_call(
        functools.partial(_combine_kernel, tm=tm, final=final),
        grid=(T // tm,),
        in_specs=[pl.BlockSpec(memory_space=pl.ANY), pl.BlockSpec((tm, D_MODEL), lambda i: (i, 0)), g2spec,
                  pl.BlockSpec((tm, SUBLANES), lambda i: (i, 0)), pl.BlockSpec((1, D_MODEL), lambda i: (0, 0)),
                  pl.BlockSpec(memory_space=pl.ANY)],
        out_specs=pl.BlockSpec((tm, D_MODEL), lambda i: (i, 0)),
        out_shape=jax.ShapeDtypeStruct((T, D_MODEL), F32),
        scratch_shapes=[pltpu.SMEM((2, 2 * tm), jnp.int32), pltpu.SemaphoreType.DMA((2,)),
                        pltpu.VMEM((2, tm, D_MODEL), F32), pltpu.SemaphoreType.DMA((1,))],
        compiler_params=_cparams("arbitrary"),
        name="moe_combine",
    )(dest2, x1, mod3, route, final_g, ybuf)


def _moe(x1, h2, route, counts, mod3, lw, final_g, *, seq, tm, per_seq, blk, final):
    T = x1.shape[0]
    cnt = counts[0, :N_EXPERTS].astype(jnp.int32)
    padded = ((cnt + blk - 1) // blk) * blk
    pend = jnp.cumsum(padded)
    pstart = pend - padded
    nb = (2 * T + blk - 1) // blk + N_EXPERTS
    eid = route[:, 0:2].astype(jnp.int32)
    rank = route[:, 2:4].astype(jnp.int32)
    dest = pstart[eid] + rank
    dest2 = jnp.transpose(dest.reshape(T // tm, tm, 2), (0, 2, 1)).reshape(T // tm, 2 * tm)
    block_e = jnp.clip(jnp.searchsorted(pend, jnp.arange(nb, dtype=jnp.int32) * blk, side="right"),
                       0, N_EXPERTS - 1).astype(jnp.int32)
    n_used = (pend[-1:] // blk).astype(jnp.int32)
    xbuf = _dispatch(h2, dest2, nb * blk, tm=tm)
    ybuf = _ffn(xbuf, block_e, n_used, lw, blk=blk)
    return _combine(x1, mod3, route, ybuf, dest2, final_g, seq=seq, tm=tm, per_seq=per_seq, final=final)


def _rope_tables(pos0, n):
    pos = (pos0 + jnp.arange(n)).astype(F32)
    inv = jnp.power(ROPE_THETA, -jnp.arange(0, QK_ROPE, 2, dtype=F32) / QK_ROPE)
    ang = pos[:, None] * inv[None, :]
    cos, sin = jnp.cos(ang), jnp.sin(ang)
    c32 = jnp.concatenate([cos, cos], axis=-1)
    s32 = jnp.concatenate([-sin, sin], axis=-1)
    return {"cq": jnp.tile(c32, (1, MLA_HEADS)), "sq": jnp.tile(s32, (1, MLA_HEADS)),
            "ck": jnp.tile(c32, (1, 4)), "sk": jnp.tile(s32, (1, 4))}


def _pair_blockdiag(w):
    z = jnp.zeros_like(w[:, 0])
    top = jnp.concatenate([w[:, 0], z], axis=-1)
    bot = jnp.concatenate([z, w[:, 1]], axis=-1)
    return jnp.concatenate([top, bot], axis=-2)


def _prep_layer(l, p):
    swap = jnp.concatenate([jnp.arange(16, 32), jnp.arange(0, 16)])
    w_in = p["w_in"][l]
    o5 = 2 * A_WIDTH + SSM_WIDTH + Q_RANK + KV_RANK
    kpe_w = w_in[:, o5:o5 + QK_ROPE]
    w_in_p = jnp.concatenate([w_in[:, :o5], jnp.tile(kpe_w, (1, 4)), jnp.tile(kpe_w[:, swap], (1, 4))], axis=-1)

    sp_w = jnp.tril(p["sp_w"][l])
    gm_wcat = jnp.transpose(sp_w, (1, 0, 2)).reshape(CHUNK, A_HEADS * CHUNK)
    gm_bt = jnp.repeat(jnp.transpose(p["sp_b"][l]), A_HEAD_DIM, axis=1)
    gm_w0 = jnp.repeat(p["sp_w"][l][:, 0, 0], A_HEAD_DIM)[None, :]
    gm_b0 = jnp.repeat(p["sp_b"][l][:, 0], A_HEAD_DIM)[None, :]

    w_uq = p["w_uq"][l].reshape(Q_RANK, MLA_HEADS, QK_NOPE + QK_ROPE)
    pe = w_uq[:, :, QK_NOPE:]
    wq = jnp.concatenate([w_uq[:, :, :QK_NOPE].reshape(Q_RANK, -1), pe.reshape(Q_RANK, -1),
                          pe[:, :, swap].reshape(Q_RANK, -1)], axis=-1)
    w_uk = jnp.transpose(p["w_uk"][l], (1, 2, 0))
    wuk_bd = _pair_blockdiag(w_uk.reshape(MLA_HEADS // 2, 2, QK_NOPE, KV_RANK))
    w_uv = jnp.transpose(p["w_uv"][l], (1, 0, 2))
    wuv_bd = _pair_blockdiag(w_uv.reshape(MLA_HEADS // 2, 2, KV_RANK, V_DIM))
    wuv_flat = p["w_uv"][l].reshape(KV_RANK, MLA_HEADS * V_DIM)

    dt = jnp.exp(p["ssm_log_step"][l])[:, None]
    lam_re, lam_im = p["ssm_lam_re"][l], p["ssm_lam_im"][l]
    mag = jnp.exp(lam_re * dt)
    abar_re = mag * jnp.cos(lam_im * dt)
    abar_im = mag * jnp.sin(lam_im * dt)
    den = lam_re * lam_re + lam_im * lam_im
    nr = abar_re - 1.0
    fr = (nr * lam_re + abar_im * lam_im) / den
    fi = (abar_im * lam_re - nr * lam_im) / den
    b_re, b_im = p["ssm_b_re"][l], p["ssm_b_im"][l]
    bbar_re = fr[..., None] * b_re - fi[..., None] * b_im
    bbar_im = fr[..., None] * b_im + fi[..., None] * b_re
    eye = jnp.eye(SSM_GROUPS, dtype=F32)

    def bd_in(bb):
        return jnp.einsum("gpc,gh->gchp", bb, eye).reshape(SSM_WIDTH, SSM_FLAT)

    def bd_out(cc):
        return jnp.einsum("gcp,gh->hpgc", cc, eye).reshape(SSM_FLAT, SSM_WIDTH)

    bbd = jnp.concatenate([bd_in(bbar_re), bd_in(bbar_im)], axis=-1)
    cbd = jnp.concatenate([bd_out(p["ssm_c_re"][l]), -bd_out(p["ssm_c_im"][l])], axis=0)

    wr = jnp.zeros((D_MODEL, LANES), F32)
    wr = wr.at[:, 0:N_GROUPS].set(p["router_g_w"][l]).at[:, N_GROUPS:N_GROUPS + N_EXPERTS].set(p["router_e_w"][l])
    br = jnp.zeros((1, LANES), F32)
    br = br.at[0, 0:N_GROUPS].set(p["router_g_b"][l]).at[0, N_GROUPS:N_GROUPS + N_EXPERTS].set(p["router_e_b"][l])
    wr_hi = wr.astype(BF16)
    wr_lo = (wr - wr_hi.astype(F32)).astype(BF16)

    return {
        "norm1_g": p["norm1_g"][l][None, :], "norm2_g": p["norm2_g"][l][None, :],
        "w_in": w_in_p.astype(BF16),
        "gm_wcat": gm_wcat.astype(BF16), "gm_bt": gm_bt, "gm_w0": gm_w0, "gm_b0": gm_b0,
        "q_norm_g": p["q_norm_g"][l][None, :], "wq": wq.astype(BF16), "wuk_bd": wuk_bd.astype(BF16),
        "kv_norm_g": p["kv_norm_g"][l][None, :], "wuv_bd": wuv_bd.astype(BF16), "wuv_flat": wuv_flat.astype(BF16),
        "bbd": bbd.astype(BF16), "cbd": cbd.astype(BF16),
        "a_re": abar_re.reshape(1, SSM_FLAT), "a_im": abar_im.reshape(1, SSM_FLAT),
        "ssm_d": p["ssm_d"][l][None, :], "glu_w": p["glu_w"][l].astype(BF16), "glu_b": p["glu_b"][l][None, :],
        "w_out": p["w_out"][l].astype(BF16),
        "wr_hi": wr_hi, "wr_lo": wr_lo, "br": br,
        "wg": p["exp_w_gate"][l].astype(BF16), "wu": p["exp_w_up"][l].astype(BF16),
        "wd": p["exp_w_down"][l].astype(BF16),
    }


def _consts(tms):
    rows = S5_SEQS * S5_SUB
    r = jnp.arange(rows)
    src = (r % S5_SEQS) * S5_SUB + r // S5_SEQS
    perm = (jnp.arange(rows)[None, :] == src[:, None]).astype(BF16)
    ltri = {tm: (jnp.arange(tm)[None, :] < jnp.arange(tm)[:, None]).astype(BF16) for tm in tms}
    return {"perm": perm, "perm_t": jnp.transpose(perm), "ltri": ltri}


def kernel(x_prompt, x_sample, cache_ckv, cache_kpe, state_ssm_re, state_ssm_im, page_table, c_prompt, c_sample, norm1_g, norm2_g, ada_w, ada_b, w_in, sp_w, sp_b, ssm_lam_re, ssm_lam_im, ssm_b_re, ssm_b_im, ssm_c_re, ssm_c_im, ssm_d, ssm_log_step, glu_w, glu_b, q_norm_g, w_uq, kv_norm_g, w_uk, w_uv, w_out, router_g_w, router_g_b, router_e_w, router_e_b, exp_w_gate, exp_w_up, exp_w_down, final_norm_g):
    params = dict(norm1_g=norm1_g, norm2_g=norm2_g, w_in=w_in, sp_w=sp_w, sp_b=sp_b, ssm_lam_re=ssm_lam_re,
                  ssm_lam_im=ssm_lam_im, ssm_b_re=ssm_b_re, ssm_b_im=ssm_b_im, ssm_c_re=ssm_c_re, ssm_c_im=ssm_c_im,
                  ssm_d=ssm_d, ssm_log_step=ssm_log_step, glu_w=glu_w, glu_b=glu_b, q_norm_g=q_norm_g, w_uq=w_uq,
                  kv_norm_g=kv_norm_g, w_uk=w_uk, w_uv=w_uv, w_out=w_out, router_g_w=router_g_w,
                  router_g_b=router_g_b, router_e_w=router_e_w, router_e_b=router_e_b, exp_w_gate=exp_w_gate,
                  exp_w_up=exp_w_up, exp_w_down=exp_w_down)
    depth = w_in.shape[0]
    nb_p, seq, _ = x_prompt.shape
    nb_s = x_sample.shape[0]
    assert x_sample.shape[1] == 1 and nb_p % S5_SEQS == 0 and seq % CHUNK == 0
    past_len = page_table.shape[1] * cache_ckv.shape[2]
    tm_p = min(512, seq)
    tm_s = nb_s
    blk_p = min(512, seq)
    blk_s = 128
    consts = _consts({tm_p, tm_s})
    final_g = final_norm_g[None, :]

    mod = _modulation(jnp.concatenate([c_prompt, c_sample], axis=0), ada_w, ada_b)
    rope_p = _rope_tables(0, seq)
    rope_s = {k: jnp.broadcast_to(v, (nb_s, v.shape[1])) for k, v in _rope_tables(past_len, 1).items()}

    xp = x_prompt.reshape(nb_p * seq, D_MODEL)
    xs = x_sample.reshape(nb_s, D_MODEL)
    outs = {k: [] for k in ("p_ckv", "p_kpe", "p_sre", "p_sim", "s_ckv", "s_kpe", "s_sre", "s_sim", "s_v")}
    for l in range(depth):
        lw = _prep_layer(l, params)
        last = l == depth - 1
        mod_p = mod[l, :nb_p].reshape(nb_p, 1, 6 * D_MODEL)
        ya, bu, qlat, qpe, ckv, kpe, kcat, _ = _proj(xp, mod_p, lw, rope_p, seq=seq, tm=tm_p, chunked=True)
        yb, sre, sim = _s5_prompt(bu, lw, consts, batch=nb_p, seq=seq)
        yc = _attn_prompt(qlat, qpe, kcat, lw, batch=nb_p, seq=seq)
        x1, h2, route, counts = _post(xp, ya, yb, yc, mod_p, lw, consts, seq=seq, tm=tm_p, per_seq=True)
        xp = _moe(x1, h2, route, counts, mod_p, lw, final_g, seq=seq, tm=tm_p, per_seq=True, blk=blk_p, final=last)
        outs["p_ckv"].append(ckv.reshape(nb_p, seq, KV_RANK))
        outs["p_kpe"].append(kpe.reshape(nb_p, seq, QK_ROPE))
        outs["p_sre"].append(sre.reshape(nb_p, SSM_GROUPS, SSM_STATE))
        outs["p_sim"].append(sim.reshape(nb_p, SSM_GROUPS, SSM_STATE))

        mod_s = mod[l, nb_p:]
        ya, bu, qlat, qpe, ckv, kpe, kcat, av = _proj(xs, mod_s, lw, rope_s, seq=1, tm=tm_s, chunked=False)
        yb, sre, sim = _s5_step(bu, state_ssm_re[l].reshape(nb_s, SSM_FLAT), state_ssm_im[l].reshape(nb_s, SSM_FLAT),
                                lw)
        yc = _attn_sample(qlat, qpe, ckv, kpe, page_table, cache_ckv, cache_kpe, lw, layer=l)
        x1, h2, route, counts = _post(xs, ya, yb, yc, mod_s, lw, consts, seq=1, tm=tm_s, per_seq=False)
        xs = _moe(x1, h2, route, counts, mod_s, lw, final_g, seq=1, tm=tm_s, per_seq=False, blk=blk_s, final=last)
        outs["s_ckv"].append(ckv.reshape(nb_s, 1, KV_RANK))
        outs["s_kpe"].append(kpe.reshape(nb_s, 1, QK_ROPE))
        outs["s_sre"].append(sre.reshape(nb_s, SSM_GROUPS, SSM_STATE))
        outs["s_sim"].append(sim.reshape(nb_s, SSM_GROUPS, SSM_STATE))
        outs["s_v"].append(av.reshape(nb_s, 1, A_WIDTH))

    return (xp.reshape(nb_p, seq, D_MODEL), xs.reshape(nb_s, 1, D_MODEL),
            jnp.stack(outs["p_ckv"]), jnp.stack(outs["p_kpe"]), jnp.stack(outs["p_sre"]), jnp.stack(outs["p_sim"]),
            jnp.stack(outs["s_ckv"]), jnp.stack(outs["s_kpe"]), jnp.stack(outs["s_sre"]), jnp.stack(outs["s_sim"]),
            jnp.stack(outs["s_v"]))
```

```python
import functools
import math

import jax
import jax.numpy as jnp
from jax import lax
from jax.experimental import pallas as pl
from jax.experimental.pallas import tpu as pltpu

F32 = jnp.float32
BF16 = jnp.bfloat16

D_MODEL = 1024
RMS_EPS = 1e-6
A_HEADS = 4
A_HEAD_DIM = 64
A_WIDTH = 256
CHUNK = 128
SSM_CH = 16
SSM_WIDTH = 256
SSM_GROUPS = 16
SSM_STATE = 64
SSM_FLAT = SSM_GROUPS * SSM_STATE
MLA_HEADS = 8
QK_NOPE = 64
QK_ROPE = 32
V_DIM = 64
Q_RANK = 256
KV_RANK = 128
MLA_WIDTH = 512
ROPE_THETA = 10000.0
ATTN_SCALE = (QK_NOPE + QK_ROPE) ** -0.5
N_GROUPS = 4
EXPERTS_PER_GROUP = 8
N_EXPERTS = 32
D_EXPERT = 512
KCAT = 256
IN_PAD = 1408

LANES = 128
SUBLANES = 8
VMEM_LIMIT = 56 * 1024 * 1024

NEG = -1e30


def _cparams(*sem):
    return pltpu.CompilerParams(dimension_semantics=tuple(sem), vmem_limit_bytes=VMEM_LIMIT)


def _dot(a, b):
    return jnp.dot(a, b, preferred_element_type=F32)


def _dot_nt(a, b):
    return lax.dot_general(a, b, (((1,), (1,)), ((), ())), preferred_element_type=F32)


def _rms(x):
    return x * lax.rsqrt(jnp.mean(x * x, axis=-1, keepdims=True) + RMS_EPS)


def _mod_kernel(c_ref, w_ref, b_ref, o_ref):
    c = c_ref[...]
    a = (c * jax.nn.sigmoid(c)).astype(BF16)
    o_ref[...] = _dot(a, w_ref[...].astype(BF16)) + b_ref[...]


def _modulation(c_all, ada_w, ada_b):
    depth = ada_w.shape[0]
    nb = c_all.shape[0]
    n_out = ada_w.shape[2]
    tn = D_MODEL
    return pl.pallas_call(
        _mod_kernel,
        grid=(depth, n_out // tn),
        in_specs=[
            pl.BlockSpec((nb, D_MODEL), lambda l, n: (0, 0)),
            pl.BlockSpec((None, D_MODEL, tn), lambda l, n: (l, 0, n)),
            pl.BlockSpec((None, 1, tn), lambda l, n: (l, 0, n)),
        ],
        out_specs=pl.BlockSpec((None, nb, tn), lambda l, n: (l, 0, n)),
        out_shape=jax.ShapeDtypeStruct((depth, nb, n_out), F32),
        compiler_params=_cparams("arbitrary", "arbitrary"),
        name="ada_mod",
    )(c_all, ada_w, ada_b.reshape(depth, 1, n_out))


def _proj_kernel(x_ref, sh_ref, sc_ref, g1_ref, win_ref, gma_ref, gmb_ref, qg_ref, wq_ref, wuk_ref, kvg_ref,
                 cq_ref, sq_ref, ck_ref, sk_ref,
                 ya_ref, bu_ref, qlat_ref, qpe_ref, ckv_ref, kpe_ref, kcat_ref, av_ref, *, chunked, tm):
    x = x_ref[...]
    h = _rms(x) * g1_ref[...] * (1.0 + sc_ref[...]) + sh_ref[...]
    proj = _dot(h.astype(BF16), win_ref[...])
    a_u = proj[:, 0:256]
    a_v = proj[:, 256:512]
    bu_ref[...] = proj[:, 512:768]
    c_q = proj[:, 768:1024]
    c_kv = proj[:, 1024:1152]
    kp4 = proj[:, 1152:1280]
    kp4s = proj[:, 1280:1408]
    av_ref[...] = a_v

    if chunked:
        lane_head = lax.broadcasted_iota(jnp.int32, (CHUNK, A_WIDTH), 1) // A_HEAD_DIM
        for c in range(tm // CHUNK):
            vb = a_v[c * CHUNK:(c + 1) * CHUNK].astype(BF16)
            vstack = jnp.concatenate(
                [jnp.where(lane_head == hd, vb, jnp.zeros_like(vb)) for hd in range(A_HEADS)], axis=0)
            gate = _dot(gma_ref[...], vstack) + gmb_ref[...]
            ya_ref[c * CHUNK:(c + 1) * CHUNK, :] = (a_u[c * CHUNK:(c + 1) * CHUNK] * gate).astype(BF16)
    else:
        ya_ref[...] = (a_u * (gma_ref[...] * a_v + gmb_ref[...])).astype(BF16)

    cq = (_rms(c_q) * qg_ref[...]).astype(BF16)
    qall = _dot(cq, wq_ref[...])
    qpe_ref[...] = (qall[:, 512:768] * cq_ref[...] + qall[:, 768:1024] * sq_ref[...]).astype(BF16)
    for j in range(MLA_HEADS // 2):
        qn = qall[:, j * 128:(j + 1) * 128].astype(BF16)
        qlat_ref[:, j * 256:(j + 1) * 256] = _dot(qn, wuk_ref[j]).astype(BF16)
    ckv = _rms(c_kv) * kvg_ref[...]
    ckv_ref[...] = ckv
    kpe4 = kp4 * ck_ref[...] + kp4s * sk_ref[...]
    kpe_ref[...] = kpe4[:, 0:QK_ROPE]
    kcat_ref[:, 0:KV_RANK] = ckv.astype(BF16)
    kcat_ref[:, KV_RANK:KCAT] = kpe4.astype(BF16)


def _proj(x, mod3, lw, rope, *, seq, tm, chunked):
    T = x.shape[0]
    nt = T // tm
    if chunked:
        per = seq // tm

        def modspec(k):
            return pl.BlockSpec((None, 1, D_MODEL), lambda i: (i // per, 0, k))

        def ropespec(w):
            return pl.BlockSpec((tm, w), lambda i: (i % per, 0))
        gma, gmb = lw["gm_wcat"], lw["gm_bt"]
    else:
        def modspec(k):
            return pl.BlockSpec((tm, D_MODEL), lambda i: (i, k))

        def ropespec(w):
            return pl.BlockSpec((tm, w), lambda i: (i, 0))
        gma, gmb = lw["gm_w0"], lw["gm_b0"]

    def full(a):
        nd = a.ndim
        return pl.BlockSpec(a.shape, lambda i: (0,) * nd)

    def rows(w):
        return pl.BlockSpec((tm, w), lambda i: (i, 0))

    outs = [
        jax.ShapeDtypeStruct((T, A_WIDTH), BF16),
        jax.ShapeDtypeStruct((T, SSM_WIDTH), F32),
        jax.ShapeDtypeStruct((T, MLA_HEADS * KV_RANK), BF16),
        jax.ShapeDtypeStruct((T, MLA_HEADS * QK_ROPE), BF16),
        jax.ShapeDtypeStruct((T, KV_RANK), F32),
        jax.ShapeDtypeStruct((T, QK_ROPE), F32),
        jax.ShapeDtypeStruct((T, KCAT), BF16),
        jax.ShapeDtypeStruct((T, A_WIDTH), F32),
    ]
    return pl.pallas_call(
        functools.partial(_proj_kernel, chunked=chunked, tm=tm),
        grid=(nt,),
        in_specs=[rows(D_MODEL), modspec(0), modspec(1), full(lw["norm1_g"]), full(lw["w_in"]), full(gma), full(gmb),
                  full(lw["q_norm_g"]), full(lw["wq"]), full(lw["wuk_bd"]), full(lw["kv_norm_g"]),
                  ropespec(256), ropespec(256), ropespec(128), ropespec(128)],
        out_specs=[rows(o.shape[1]) for o in outs],
        out_shape=outs,
        compiler_params=_cparams("arbitrary"),
        name="proj",
    )(x, mod3, mod3, lw["norm1_g"], lw["w_in"], gma, gmb, lw["q_norm_g"], lw["wq"], lw["wuk_bd"], lw["kv_norm_g"],
      rope["cq"], rope["sq"], rope["ck"], rope["sk"])


def _gelu_glu(y, gw_ref, gb_ref):
    z = jax.nn.gelu(y)
    gate = jax.nn.sigmoid(_dot(z.astype(BF16), gw_ref[...]) + gb_ref[...])
    return z * gate


S5_SUB = 32
S5_SEQS = 8


def _s5_prompt_kernel(u_ref, p_ref, pt_ref, bbd_ref, ar_ref, ai_ref, cbd_ref, d_ref, gw_ref, gb_ref,
                      yb_ref, sre_ref, sim_ref, bu_s, xs_s, st_s, *, nsub):
    i = pl.program_id(1)

    @pl.when(i == 0)
    def _():
        st_s[...] = jnp.zeros_like(st_s)

    rows = S5_SEQS * S5_SUB
    ar = jnp.broadcast_to(ar_ref[...], (S5_SEQS, SSM_FLAT))
    ai = jnp.broadcast_to(ai_ref[...], (S5_SEQS, SSM_FLAT))
    for k in range(nsub):
        u = u_ref[:, k * S5_SUB:(k + 1) * S5_SUB, :].reshape(rows, SSM_WIDTH)
        hi = u.astype(BF16)
        r1 = u - hi.astype(F32)
        mid = r1.astype(BF16)
        lo = (r1 - mid.astype(F32)).astype(BF16)
        up_hi = _dot(p_ref[...], hi)
        up = up_hi + _dot(p_ref[...], mid) + _dot(p_ref[...], lo)
        bu_s[...] = _dot(up_hi.astype(BF16), bbd_ref[...])

        def step(t, x):
            r = pl.multiple_of(t * S5_SEQS, S5_SEQS)
            b = bu_s[pl.ds(r, S5_SEQS), :]
            xr = x[:, :SSM_FLAT]
            xi = x[:, SSM_FLAT:]
            nr = ar * xr - ai * xi + b[:, :SSM_FLAT]
            ni = ar * xi + ai * xr + b[:, SSM_FLAT:]
            xn = jnp.concatenate([nr, ni], axis=-1)
            xs_s[pl.ds(r, S5_SEQS), :] = xn
            return xn

        x = lax.fori_loop(0, S5_SUB, step, st_s[...], unroll=4)
        st_s[...] = x
        y = _dot(xs_s[...].astype(BF16), cbd_ref[...]) + d_ref[...] * up
        yb = _gelu_glu(y, gw_ref, gb_ref).astype(BF16)
        back = _dot(pt_ref[...], yb).astype(BF16)
        yb_ref[:, k * S5_SUB:(k + 1) * S5_SUB, :] = back.reshape(S5_SEQS, S5_SUB, SSM_WIDTH)

    @pl.when(i == pl.num_programs(1) - 1)
    def _():
        sre_ref[...] = st_s[:, :SSM_FLAT]
        sim_ref[...] = st_s[:, SSM_FLAT:]


def _s5_prompt(bu, lw, consts, *, batch, seq):
    tt = min(128, seq)
    nsub = tt // S5_SUB
    u3 = bu.reshape(batch, seq, SSM_WIDTH)

    def full(a):
        nd = a.ndim
        return pl.BlockSpec(a.shape, lambda j, i: (0,) * nd)

    rows = S5_SEQS * S5_SUB
    yb, sre, sim = pl.pallas_call(
        functools.partial(_s5_prompt_kernel, nsub=nsub),
        grid=(batch // S5_SEQS, seq // tt),
        in_specs=[pl.BlockSpec((S5_SEQS, tt, SSM_WIDTH), lambda j, i: (j, i, 0)),
                  full(consts["perm"]), full(consts["perm_t"]), full(lw["bbd"]), full(lw["a_re"]), full(lw["a_im"]),
                  full(lw["cbd"]), full(lw["ssm_d"]), full(lw["glu_w"]), full(lw["glu_b"])],
        out_specs=[pl.BlockSpec((S5_SEQS, tt, SSM_WIDTH), lambda j, i: (j, i, 0)),
                   pl.BlockSpec((S5_SEQS, SSM_FLAT), lambda j, i: (j, 0)),
                   pl.BlockSpec((S5_SEQS, SSM_FLAT), lambda j, i: (j, 0))],
        out_shape=[jax.ShapeDtypeStruct((batch, seq, SSM_WIDTH), BF16),
                   jax.ShapeDtypeStruct((batch, SSM_FLAT), F32),
                   jax.ShapeDtypeStruct((batch, SSM_FLAT), F32)],
        scratch_shapes=[pltpu.VMEM((rows, 2 * SSM_FLAT), F32), pltpu.VMEM((rows, 2 * SSM_FLAT), F32),
                        pltpu.VMEM((S5_SEQS, 2 * SSM_FLAT), F32)],
        compiler_params=_cparams("arbitrary", "arbitrary"),
        name="s5_prompt",
    )(u3, consts["perm"], consts["perm_t"], lw["bbd"], lw["a_re"], lw["a_im"], lw["cbd"], lw["ssm_d"],
      lw["glu_w"], lw["glu_b"])
    return yb.reshape(batch * seq, SSM_WIDTH), sre, sim


def _s5_step_kernel(u_ref, x0r_ref, x0i_ref, bbd_ref, ar_ref, ai_ref, cbd_ref, d_ref, gw_ref, gb_ref,
                    yb_ref, sre_ref, sim_ref):
    u = u_ref[...]
    bu = _dot(u.astype(BF16), bbd_ref[...])
    ar = ar_ref[...]
    ai = ai_ref[...]
    xr = x0r_ref[...]
    xi = x0i_ref[...]
    nr = ar * xr - ai * xi + bu[:, :SSM_FLAT]
    ni = ar * xi + ai * xr + bu[:, SSM_FLAT:]
    sre_ref[...] = nr
    sim_ref[...] = ni
    xcat = jnp.concatenate([nr, ni], axis=-1).astype(BF16)
    y = _dot(xcat, cbd_ref[...]) + d_ref[...] * u
    yb_ref[...] = _gelu_glu(y, gw_ref, gb_ref).astype(BF16)


def _s5_step(bu, x0r, x0i, lw):
    n = bu.shape[0]
    args = (bu, x0r, x0i, lw["bbd"], lw["a_re"], lw["a_im"], lw["cbd"], lw["ssm_d"], lw["glu_w"], lw["glu_b"])
    return pl.pallas_call(
        _s5_step_kernel,
        out_shape=[jax.ShapeDtypeStruct((n, SSM_WIDTH), BF16), jax.ShapeDtypeStruct((n, SSM_FLAT), F32),
                   jax.ShapeDtypeStruct((n, SSM_FLAT), F32)],
        compiler_params=pltpu.CompilerParams(vmem_limit_bytes=VMEM_LIMIT),
        name="s5_step",
    )(*args)


EXP2_SCALE = ATTN_SCALE * math.log2(math.e)


def _attn_prompt_kernel(qlat_ref, qpe_ref, k_ref, wuv_ref, yc_ref, qs_s, m_s, acc_s, *, tq):
    i = pl.program_id(1)
    nr = MLA_HEADS * tq
    lane_head = lax.broadcasted_iota(jnp.int32, (tq, LANES), 1) // QK_ROPE
    for hd in range(MLA_HEADS):
        qs_s[hd * tq:(hd + 1) * tq, 0:KV_RANK] = qlat_ref[:, hd * KV_RANK:(hd + 1) * KV_RANK]
        grp = qpe_ref[:, (hd // 4) * LANES:(hd // 4 + 1) * LANES]
        qs_s[hd * tq:(hd + 1) * tq, KV_RANK:KCAT] = jnp.where(lane_head == hd % 4, grp, jnp.zeros_like(grp))
    m_s[...] = jnp.full((nr, LANES), NEG, F32)
    acc_s[...] = jnp.zeros((nr, 2 * KV_RANK), F32)
    ones = jnp.ones((tq, KV_RANK), BF16)

    ncol = tq // LANES

    def tile(j, masked):
        k0 = pl.multiple_of(j * tq, tq)
        kt = k_ref[pl.ds(k0, tq), :]
        v = jnp.concatenate([kt[:, 0:KV_RANK], ones], axis=-1)
        for hd in range(MLA_HEADS):
            rs = slice(hd * tq, (hd + 1) * tq)
            s = _dot_nt(qs_s[rs, :], kt) * EXP2_SCALE
            if masked:
                causal = (lax.broadcasted_iota(jnp.int32, (tq, tq), 1)
                          <= lax.broadcasted_iota(jnp.int32, (tq, tq), 0))
                s = jnp.where(causal, s, NEG)
            parts = [s[:, c * LANES:(c + 1) * LANES] for c in range(ncol)]
            mx = parts[0]
            for part in parts[1:]:
                mx = jnp.maximum(mx, part)
            m_old = m_s[rs, :]
            m_new = jnp.maximum(m_old, jnp.max(mx, axis=-1, keepdims=True))
            alpha = jnp.exp2(m_old - m_new)
            m_s[rs, :] = m_new
            p = jnp.concatenate([jnp.exp2(part - m_new).astype(BF16) for part in parts], axis=-1)
            acc_s[rs, :] = jnp.concatenate([alpha, alpha], axis=-1) * acc_s[rs, :] + _dot(p, v)

    def off_diag(j, carry):
        tile(j, False)
        return carry

    lax.fori_loop(0, i, off_diag, 0)
    tile(i, True)
    o = acc_s[:, 0:KV_RANK] / acc_s[:, KV_RANK:2 * KV_RANK]
    for j in range(MLA_HEADS // 2):
        pair = jnp.concatenate([o[(2 * j) * tq:(2 * j + 1) * tq], o[(2 * j + 1) * tq:(2 * j + 2) * tq]], axis=-1)
        yc_ref[:, j * 128:(j + 1) * 128] = _dot(pair.astype(BF16), wuv_ref[j]).astype(BF16)


def _attn_prompt(qlat, qpe, kcat, lw, *, batch, seq):
    tq = min(256, seq)
    nr = MLA_HEADS * tq
    yc = pl.pallas_call(
        functools.partial(_attn_prompt_kernel, tq=tq),
        grid=(batch, seq // tq),
        in_specs=[pl.BlockSpec((None, tq, MLA_HEADS * KV_RANK), lambda b, i: (b, i, 0)),
                  pl.BlockSpec((None, tq, MLA_HEADS * QK_ROPE), lambda b, i: (b, i, 0)),
                  pl.BlockSpec((None, seq, KCAT), lambda b, i: (b, 0, 0)),
                  pl.BlockSpec(lw["wuv_bd"].shape, lambda b, i: (0, 0, 0))],
        out_specs=pl.BlockSpec((None, tq, MLA_WIDTH), lambda b, i: (b, i, 0)),
        out_shape=jax.ShapeDtypeStruct((batch, seq, MLA_WIDTH), BF16),
        scratch_shapes=[pltpu.VMEM((nr, KCAT), BF16), pltpu.VMEM((nr, LANES), F32),
                        pltpu.VMEM((nr, 2 * KV_RANK), F32)],
        compiler_params=_cparams("arbitrary", "arbitrary"),
        name="attn_prompt",
    )(qlat.reshape(batch, seq, -1), qpe.reshape(batch, seq, -1), kcat.reshape(batch, seq, KCAT), lw["wuv_bd"])
    return yc.reshape(batch * seq, MLA_WIDTH)


def _attn_sample_kernel(pt_ref, ql_ref, qp_ref, kn_ref, pn_ref, wuv_ref, ckv_hbm, kpet_hbm, yc_ref,
                        cbuf, pbuf, sem, m_s, l_s, acc_s, *, layer, pc, nchunk, page):
    b = pl.program_id(0)
    c = pl.program_id(1)
    step = b * nchunk + c
    nstep = pl.num_programs(0) * nchunk
    slot = step % 2

    def page_copies(bb, cc, sl, p):
        pg = pt_ref[bb, cc * pc + p]
        off = pl.multiple_of(p * page, page)
        return (pltpu.make_async_copy(ckv_hbm.at[layer, pg], cbuf.at[sl, p], sem.at[sl, 0]),
                pltpu.make_async_copy(kpet_hbm.at[layer, pg], pbuf.at[sl, :, pl.ds(off, page)], sem.at[sl, 1]))

    def start_chunk(bb, cc, sl):
        def body(p, carry):
            for cp in page_copies(bb, cc, sl, p):
                cp.start()
            return carry
        lax.fori_loop(0, pc, body, 0)

    def wait_chunk(bb, cc, sl):
        def body(p, carry):
            for cp in page_copies(bb, cc, sl, p):
                cp.wait()
            return carry
        lax.fori_loop(0, pc, body, 0)

    @pl.when(step == 0)
    def _():
        start_chunk(b, c, slot)

    @pl.when(step + 1 < nstep)
    def _():
        nxt = step + 1
        start_chunk(nxt // nchunk, nxt % nchunk, 1 - slot)

    ql = ql_ref[...]
    qp = qp_ref[...]

    @pl.when(c == 0)
    def _():
        kn = kn_ref[...].astype(BF16).astype(F32)
        pn = pn_ref[...].astype(BF16).astype(F32)
        s_new = (jnp.sum(ql.astype(F32) * kn, axis=-1, keepdims=True)
                 + jnp.sum(qp.astype(F32) * pn, axis=-1, keepdims=True)) * ATTN_SCALE
        m_s[...] = s_new
        l_s[...] = jnp.ones_like(l_s)
        acc_s[...] = jnp.broadcast_to(kn, acc_s.shape)

    wait_chunk(b, c, slot)

    kc = cbuf[slot].reshape(pc * page, KV_RANK).astype(BF16)
    kpt = pbuf[slot].astype(BF16)
    s = (_dot_nt(ql, kc) + _dot(qp, kpt)) * ATTN_SCALE
    m_old = m_s[...]
    m_new = jnp.maximum(m_old, jnp.max(s, axis=-1, keepdims=True))
    alpha = jnp.exp(m_old - m_new)
    p = jnp.exp(s - m_new)
    l_s[...] = alpha * l_s[...] + jnp.sum(p, axis=-1, keepdims=True)
    acc_s[...] = alpha * acc_s[...] + _dot(p.astype(BF16), kc)
    m_s[...] = m_new

    @pl.when(c == nchunk - 1)
    def _():
        o = (acc_s[...] / l_s[...]).astype(BF16)
        yfull = _dot(o, wuv_ref[...])
        sel = lax.broadcasted_iota(jnp.int32, yfull.shape, 1) // V_DIM == lax.broadcasted_iota(
            jnp.int32, yfull.shape, 0)
        yc_ref[...] = jnp.sum(jnp.where(sel, yfull, 0.0), axis=0, keepdims=True).astype(BF16)


SAMPLE_CHUNK_ROWS = 16384


def _attn_sample(qlat, qpe, ckv_new, kpe_new, page_table, cache_ckv, cache_kpet, lw, *, layer):
    n, npages = page_table.shape
    page = cache_ckv.shape[2]
    assert page % LANES == 0
    pc = max(1, SAMPLE_CHUNK_ROWS // page)
    while npages % pc:
        pc -= 1
    nchunk = npages // pc
    ql = qlat.reshape(n, MLA_HEADS, KV_RANK)
    qp = qpe.reshape(n, MLA_HEADS, QK_ROPE)
    grid_spec = pltpu.PrefetchScalarGridSpec(
        num_scalar_prefetch=1,
        grid=(n, nchunk),
        in_specs=[pl.BlockSpec((None, MLA_HEADS, KV_RANK), lambda b, c, pt: (b, 0, 0)),
                  pl.BlockSpec((None, MLA_HEADS, QK_ROPE), lambda b, c, pt: (b, 0, 0)),
                  pl.BlockSpec((None, 1, KV_RANK), lambda b, c, pt: (b, 0, 0)),
                  pl.BlockSpec((None, 1, QK_ROPE), lambda b, c, pt: (b, 0, 0)),
                  pl.BlockSpec(lw["wuv_flat"].shape, lambda b, c, pt: (0, 0)),
                  pl.BlockSpec(memory_space=pl.ANY),
                  pl.BlockSpec(memory_space=pl.ANY)],
        out_specs=pl.BlockSpec((None, 1, MLA_WIDTH), lambda b, c, pt: (b, 0, 0)),
        scratch_shapes=[pltpu.VMEM((2, pc, page, KV_RANK), F32), pltpu.VMEM((2, QK_ROPE, pc * page), F32),
                        pltpu.SemaphoreType.DMA((2, 2)),
                        pltpu.VMEM((MLA_HEADS, 1), F32), pltpu.VMEM((MLA_HEADS, 1), F32),
                        pltpu.VMEM((MLA_HEADS, KV_RANK), F32)],
    )
    yc = pl.pallas_call(
        functools.partial(_attn_sample_kernel, layer=layer, pc=pc, nchunk=nchunk, page=page),
        grid_spec=grid_spec,
        out_shape=jax.ShapeDtypeStruct((n, 1, MLA_WIDTH), BF16),
        compiler_params=_cparams("arbitrary", "arbitrary"),
        name="attn_sample",
    )(page_table, ql, qp, ckv_new.reshape(n, 1, KV_RANK), kpe_new.reshape(n, 1, QK_ROPE), lw["wuv_flat"],
      cache_ckv, cache_kpet)
    return yc.reshape(n, MLA_WIDTH)


def _post_kernel(x_ref, ya_ref, yb_ref, yc_ref, wo_ref, g1_ref, n2_ref, sc2_ref, sh2_ref, wrh_ref, wrl_ref, br_ref,
                 ltri_ref, x1_ref, h2_ref, route_ref, cnt_ref, run_s):
    i = pl.program_id(0)

    @pl.when(i == 0)
    def _():
        run_s[...] = jnp.zeros_like(run_s)

    y = (_dot(ya_ref[...], wo_ref[0:256, :]) + _dot(yb_ref[...], wo_ref[256:512, :])
         + _dot(yc_ref[...], wo_ref[512:1024, :]))
    x1 = x_ref[...] + g1_ref[...] * y
    x1_ref[...] = x1
    h2 = _rms(x1) * n2_ref[...] * (1.0 + sc2_ref[...]) + sh2_ref[...]
    h2_ref[...] = _pack_rows(h2)

    hh = h2.astype(BF16)
    hl = (h2 - hh.astype(F32)).astype(BF16)
    logits = _dot(hh, wrh_ref[...]) + _dot(hh, wrl_ref[...]) + _dot(hl, wrh_ref[...]) + br_ref[...]
    tm = logits.shape[0]
    lane = lax.broadcasted_iota(jnp.int32, (tm, LANES), 1).astype(F32)
    big = float(LANES)

    gl = jnp.where(lane < N_GROUPS, logits, NEG)
    gmax = jnp.max(gl, axis=-1, keepdims=True)
    gidx = jnp.min(jnp.where(gl == gmax, lane, big), axis=-1, keepdims=True)
    gden = jnp.sum(jnp.where(lane < N_GROUPS, jnp.exp(gl - gmax), 0.0), axis=-1, keepdims=True)
    gprob = 1.0 / gden

    lo = N_GROUPS + EXPERTS_PER_GROUP * gidx
    el = jnp.where((lane >= lo) & (lane < lo + EXPERTS_PER_GROUP), logits, NEG)
    m1 = jnp.max(el, axis=-1, keepdims=True)
    i1 = jnp.min(jnp.where(el == m1, lane, big), axis=-1, keepdims=True)
    el2 = jnp.where(lane == i1, NEG, el)
    m2 = jnp.max(el2, axis=-1, keepdims=True)
    i2 = jnp.min(jnp.where(el2 == m2, lane, big), axis=-1, keepdims=True)
    e21 = jnp.exp(m2 - m1)
    w1 = gprob / (1.0 + e21)
    w2 = gprob * e21 / (1.0 + e21)
    e1 = i1 - N_GROUPS
    e2 = i2 - N_GROUPS

    oh1 = lane == e1
    oh2 = lane == e2
    both = jnp.where(oh1, 1.0, 0.0) + jnp.where(oh2, 1.0, 0.0)
    tot = _dot(ltri_ref[...], both.astype(BF16)) + run_s[...]
    r1 = jnp.sum(jnp.where(oh1, tot, 0.0), axis=-1, keepdims=True)
    r2 = jnp.sum(jnp.where(oh2, tot, 0.0), axis=-1, keepdims=True)
    run = run_s[...] + jnp.sum(both, axis=0, keepdims=True)
    run_s[...] = run
    cnt_ref[...] = jnp.broadcast_to(run, cnt_ref.shape)

    route = jnp.where(lane == 0, e1, jnp.where(lane == 1, e2, jnp.where(lane == 2, r1, jnp.where(
        lane == 3, r2, jnp.where(lane == 4, w1, jnp.where(lane == 5, w2, 0.0))))))
    route_ref[...] = route[:, 0:SUBLANES]


def _post(x, ya, yb, yc, mod3, lw, consts, *, seq, tm, per_seq):
    T = x.shape[0]
    nt = T // tm
    if per_seq:
        per = seq // tm

        def modspec(k):
            return pl.BlockSpec((None, 1, D_MODEL), lambda i: (i // per, 0, k))
    else:
        def modspec(k):
            return pl.BlockSpec((tm, D_MODEL), lambda i: (i, k))

    def full(a):
        nd = a.ndim
        return pl.BlockSpec(a.shape, lambda i: (0,) * nd)

    def rows(w):
        return pl.BlockSpec((tm, w), lambda i: (i, 0))

    ltri = consts["ltri"][tm]
    return pl.pallas_call(
        _post_kernel,
        grid=(nt,),
        in_specs=[rows(D_MODEL), rows(A_WIDTH), rows(SSM_WIDTH), rows(MLA_WIDTH), full(lw["w_out"]),
                  modspec(2), full(lw["norm2_g"]), modspec(4), modspec(3),
                  full(lw["wr_hi"]), full(lw["wr_lo"]), full(lw["br"]), full(ltri)],
        out_specs=[rows(D_MODEL), rows(HALF), rows(SUBLANES), pl.BlockSpec((SUBLANES, LANES), lambda i: (0, 0))],
        out_shape=[jax.ShapeDtypeStruct((T, D_MODEL), F32), jax.ShapeDtypeStruct((T, HALF), jnp.uint32),
                   jax.ShapeDtypeStruct((T, SUBLANES), F32), jax.ShapeDtypeStruct((SUBLANES, LANES), F32)],
        scratch_shapes=[pltpu.VMEM((1, LANES), F32)],
        compiler_params=_cparams("arbitrary"),
        name="post",
    )(x, ya, yb, yc, lw["w_out"], mod3, lw["norm2_g"], mod3, mod3, lw["wr_hi"], lw["wr_lo"], lw["br"], ltri)


def _stage_indices(dest_hbm, idx_s, isem):
    i = pl.program_id(0)
    n = pl.num_programs(0)
    slot = i % 2

    def cp(step, sl):
        return pltpu.make_async_copy(dest_hbm.at[step], idx_s.at[sl], isem.at[sl])

    @pl.when(i == 0)
    def _():
        cp(0, 0).start()

    cp(i, slot).wait()

    @pl.when(i + 1 < n)
    def _():
        cp(i + 1, 1 - slot).start()

    return slot


HALF = D_MODEL // 2
HI_MASK = 0xFFFF0000


def _pack_rows(x):
    u = lax.bitcast_convert_type(x.astype(BF16).astype(F32), jnp.uint32)
    return (u[:, :HALF] >> 16) | (u[:, HALF:] & jnp.uint32(HI_MASK))


def _unpack_rows(w):
    lo = lax.bitcast_convert_type(w << 16, F32)
    hi = lax.bitcast_convert_type(w & jnp.uint32(HI_MASK), F32)
    return lo, hi


def _dispatch_kernel(dest_hbm, h_ref, xin_hbm, xbuf_hbm, idx_s, isem, hs, rsem, *, tm, nt):
    del xin_hbm
    i = pl.program_id(0)
    slot = _stage_indices(dest_hbm, idx_s, isem)
    hs[slot] = h_ref[...]

    def row_copy(r, d):
        return pltpu.make_async_copy(hs.at[slot, pl.ds(r, 1)], xbuf_hbm.at[pl.ds(d, 1)], rsem.at[slot])

    def issue(q, carry):
        for u in range(2):
            r = 2 * q + u
            row_copy(r, idx_s[slot, r]).start(priority=u)
            row_copy(r, idx_s[slot, tm + r]).start(priority=u)
        return carry

    lax.fori_loop(0, tm // 2, issue, 0, unroll=4)

    def drain(sl):
        for _ in range(2):
            pltpu.make_async_copy(hs.at[sl], xbuf_hbm.at[pl.ds(0, tm)], rsem.at[sl]).wait()

    @pl.when(i > 0)
    def _():
        drain(1 - slot)

    @pl.when(i == nt - 1)
    def _():
        drain(slot)


def _dispatch(h2p, dest2, n_rows, *, tm):
    T = h2p.shape[0]
    nt = T // tm
    xzero = jnp.zeros((n_rows, HALF), jnp.uint32)
    return pl.pallas_call(
        functools.partial(_dispatch_kernel, tm=tm, nt=nt),
        grid=(nt,),
        in_specs=[pl.BlockSpec(memory_space=pl.ANY), pl.BlockSpec((tm, HALF), lambda i: (i, 0)),
                  pl.BlockSpec(memory_space=pl.ANY)],
        out_specs=pl.BlockSpec(memory_space=pl.ANY),
        out_shape=jax.ShapeDtypeStruct((n_rows, HALF), jnp.uint32),
        scratch_shapes=[pltpu.SMEM((2, 2 * tm), jnp.int32), pltpu.SemaphoreType.DMA((2,)),
                        pltpu.VMEM((2, tm, HALF), jnp.uint32), pltpu.SemaphoreType.DMA((2,))],
        input_output_aliases={2: 0},
        compiler_params=_cparams("arbitrary"),
        name="moe_dispatch",
    )(dest2, h2p, xzero)


def _ffn_kernel(be_ref, nu_ref, x_ref, wg_ref, wu_ref, wd_ref, y_ref):
    i = pl.program_id(0)

    @pl.when(i < nu_ref[0])
    def _():
        lo, hi = _unpack_rows(x_ref[...])
        x = jnp.concatenate([lo, hi], axis=-1).astype(BF16)
        g = _dot(x, wg_ref[...])
        u = _dot(x, wu_ref[...])
        mid = (g * jax.nn.sigmoid(g) * u).astype(BF16)
        y_ref[...] = _pack_rows(_dot(mid, wd_ref[...]))

    @pl.when(i >= nu_ref[0])
    def _():
        y_ref[...] = jnp.zeros_like(y_ref)


def _ffn(xbuf, block_e, n_used, lw, *, blk):
    n_rows = xbuf.shape[0]
    nb = n_rows // blk
    grid_spec = pltpu.PrefetchScalarGridSpec(
        num_scalar_prefetch=2,
        grid=(nb,),
        in_specs=[pl.BlockSpec((blk, HALF), lambda i, be, nu: (i, 0)),
                  pl.BlockSpec((None, D_MODEL, D_EXPERT), lambda i, be, nu: (be[i], 0, 0)),
                  pl.BlockSpec((None, D_MODEL, D_EXPERT), lambda i, be, nu: (be[i], 0, 0)),
                  pl.BlockSpec((None, D_EXPERT, D_MODEL), lambda i, be, nu: (be[i], 0, 0))],
        out_specs=pl.BlockSpec((blk, HALF), lambda i, be, nu: (i, 0)),
    )
    return pl.pallas_call(
        _ffn_kernel,
        grid_spec=grid_spec,
        out_shape=jax.ShapeDtypeStruct((n_rows, HALF), jnp.uint32),
        compiler_params=_cparams("arbitrary"),
        name="moe_ffn",
    )(block_e, n_used, xbuf, lw["wg"], lw["wu"], lw["wd"])


def _combine_kernel(dest_hbm, x1_ref, g2_ref, route_ref, fg_ref, ybuf_hbm, out_ref, idx_s, isem, ysc, rsem,
                    *, tm, nt, final):
    i = pl.program_id(0)
    slot = i % 2

    def idx_copy(step, sl):
        return pltpu.make_async_copy(dest_hbm.at[step], idx_s.at[sl], isem.at[sl])

    def issue_tile(sl):
        def row_copy(k, r, d):
            return pltpu.make_async_copy(ybuf_hbm.at[pl.ds(d, 1)], ysc.at[sl, k, pl.ds(r, 1)], rsem.at[sl])

        def issue(q, carry):
            for u in range(2):
                r = 2 * q + u
                row_copy(0, r, idx_s[sl, r]).start(priority=u)
                row_copy(1, r, idx_s[sl, tm + r]).start(priority=u)
            return carry

        lax.fori_loop(0, tm // 2, issue, 0, unroll=4)

    @pl.when(i == 0)
    def _():
        idx_copy(0, 0).start()
        idx_copy(0, 0).wait()
        issue_tile(0)
        if nt > 1:
            idx_copy(1, 1).start()

    if nt > 1:
        @pl.when(i + 1 < nt)
        def _():
            idx_copy(i + 1, 1 - slot).wait()
            issue_tile(1 - slot)

            @pl.when(i + 2 < nt)
            def _():
                idx_copy(i + 2, slot).start()

    for k in range(2):
        pltpu.make_async_copy(ybuf_hbm.at[pl.ds(0, tm)], ysc.at[slot, k], rsem.at[slot]).wait()

    route = route_ref[...]
    w1 = route[:, 4:5]
    w2 = route[:, 5:6]
    lo1, hi1 = _unpack_rows(ysc[slot, 0])
    lo2, hi2 = _unpack_rows(ysc[slot, 1])
    ff = jnp.concatenate([lo1 * w1 + lo2 * w2, hi1 * w1 + hi2 * w2], axis=-1)
    x2 = x1_ref[...] + g2_ref[...] * ff
    if final:
        x2 = _rms(x2) * fg_ref[...]
    out_ref[...] = x2


def _combine(x1, mod3, route, ybuf, dest2, final_g, *, seq, tm, per_seq, final):
    T = x1.shape[0]
    if per_seq:
        per = seq // tm
        g2spec = pl.BlockSpec((None, 1, D_MODEL), lambda i: (i // per, 0, 5))
    else:
        g2spec = pl.BlockSpec((tm, D_MODEL), lambda i: (i, 5))
    return pl.pallas_call(
        functools.partial(_combine_kernel, tm=tm, nt=T // tm, final=final),
        grid=(T // tm,),
        in_specs=[pl.BlockSpec(memory_space=pl.ANY), pl.BlockSpec((tm, D_MODEL), lambda i: (i, 0)), g2spec,
                  pl.BlockSpec((tm, SUBLANES), lambda i: (i, 0)), pl.BlockSpec((1, D_MODEL), lambda i: (0, 0)),
                  pl.BlockSpec(memory_space=pl.ANY)],
        out_specs=pl.BlockSpec((tm, D_MODEL), lambda i: (i, 0)),
        out_shape=jax.ShapeDtypeStruct((T, D_MODEL), F32),
        scratch_shapes=[pltpu.SMEM((2, 2 * tm), jnp.int32), pltpu.SemaphoreType.DMA((2,)),
                        pltpu.VMEM((2, 2, tm, HALF), jnp.uint32), pltpu.SemaphoreType.DMA((2,))],
        compiler_params=_cparams("arbitrary"),
        name="moe_combine",
    )(dest2, x1, mod3, route, final_g, ybuf)


def _moe(x1, h2, route, counts, mod3, lw, final_g, *, seq, tm, per_seq, blk, final):
    T = x1.shape[0]
    cnt = counts[0, :N_EXPERTS].astype(jnp.int32)
    padded = ((cnt + blk - 1) // blk) * blk
    pend = jnp.cumsum(padded)
    pstart = pend - padded
    nb = (2 * T + blk - 1) // blk + N_EXPERTS
    eid = route[:, 0:2].astype(jnp.int32)
    rank = route[:, 2:4].astype(jnp.int32)
    dest = pstart[eid] + rank
    dest2 = jnp.transpose(dest.reshape(T // tm, tm, 2), (0, 2, 1)).reshape(T // tm, 2 * tm)
    starts = jnp.arange(nb, dtype=jnp.int32) * blk
    block_e = jnp.minimum(jnp.sum((pend[None, :] <= starts[:, None]).astype(jnp.int32), axis=1), N_EXPERTS - 1)
    n_used = (pend[-1:] // blk).astype(jnp.int32)
    xbuf = _dispatch(h2, dest2, nb * blk, tm=tm)
    ybuf = _ffn(xbuf, block_e, n_used, lw, blk=blk)
    return _combine(x1, mod3, route, ybuf, dest2, final_g, seq=seq, tm=tm, per_seq=per_seq, final=final)


def _rope_tables(pos0, n):
    pos = (pos0 + jnp.arange(n)).astype(F32)
    inv = jnp.power(ROPE_THETA, -jnp.arange(0, QK_ROPE, 2, dtype=F32) / QK_ROPE)
    ang = pos[:, None] * inv[None, :]
    cos, sin = jnp.cos(ang), jnp.sin(ang)
    c32 = jnp.concatenate([cos, cos], axis=-1)
    s32 = jnp.concatenate([-sin, sin], axis=-1)
    return {"cq": jnp.tile(c32, (1, MLA_HEADS)), "sq": jnp.tile(s32, (1, MLA_HEADS)),
            "ck": jnp.tile(c32, (1, 4)), "sk": jnp.tile(s32, (1, 4))}


def _pair_blockdiag(w):
    z = jnp.zeros_like(w[:, 0])
    top = jnp.concatenate([w[:, 0], z], axis=-1)
    bot = jnp.concatenate([z, w[:, 1]], axis=-1)
    return jnp.concatenate([top, bot], axis=-2)


def _prep_layer(l, p):
    swap = jnp.concatenate([jnp.arange(16, 32), jnp.arange(0, 16)])
    w_in = p["w_in"][l]
    o5 = 2 * A_WIDTH + SSM_WIDTH + Q_RANK + KV_RANK
    kpe_w = w_in[:, o5:o5 + QK_ROPE]
    w_in_p = jnp.concatenate([w_in[:, :o5], jnp.tile(kpe_w, (1, 4)), jnp.tile(kpe_w[:, swap], (1, 4))], axis=-1)

    sp_w = jnp.tril(p["sp_w"][l])
    gm_wcat = jnp.transpose(sp_w, (1, 0, 2)).reshape(CHUNK, A_HEADS * CHUNK)
    gm_bt = jnp.repeat(jnp.transpose(p["sp_b"][l]), A_HEAD_DIM, axis=1)
    gm_w0 = jnp.repeat(p["sp_w"][l][:, 0, 0], A_HEAD_DIM)[None, :]
    gm_b0 = jnp.repeat(p["sp_b"][l][:, 0], A_HEAD_DIM)[None, :]

    w_uq = p["w_uq"][l].reshape(Q_RANK, MLA_HEADS, QK_NOPE + QK_ROPE)
    pe = w_uq[:, :, QK_NOPE:]
    wq = jnp.concatenate([w_uq[:, :, :QK_NOPE].reshape(Q_RANK, -1), pe.reshape(Q_RANK, -1),
                          pe[:, :, swap].reshape(Q_RANK, -1)], axis=-1)
    w_uk = jnp.transpose(p["w_uk"][l], (1, 2, 0))
    wuk_bd = _pair_blockdiag(w_uk.reshape(MLA_HEADS // 2, 2, QK_NOPE, KV_RANK))
    w_uv = jnp.transpose(p["w_uv"][l], (1, 0, 2))
    wuv_bd = _pair_blockdiag(w_uv.reshape(MLA_HEADS // 2, 2, KV_RANK, V_DIM))
    wuv_flat = p["w_uv"][l].reshape(KV_RANK, MLA_HEADS * V_DIM)

    dt = jnp.exp(p["ssm_log_step"][l])[:, None]
    lam_re, lam_im = p["ssm_lam_re"][l], p["ssm_lam_im"][l]
    mag = jnp.exp(lam_re * dt)
    abar_re = mag * jnp.cos(lam_im * dt)
    abar_im = mag * jnp.sin(lam_im * dt)
    den = lam_re * lam_re + lam_im * lam_im
    nr = abar_re - 1.0
    fr = (nr * lam_re + abar_im * lam_im) / den
    fi = (abar_im * lam_re - nr * lam_im) / den
    b_re, b_im = p["ssm_b_re"][l], p["ssm_b_im"][l]
    bbar_re = fr[..., None] * b_re - fi[..., None] * b_im
    bbar_im = fr[..., None] * b_im + fi[..., None] * b_re
    eye = jnp.eye(SSM_GROUPS, dtype=F32)

    def bd_in(bb):
        return jnp.einsum("gpc,gh->gchp", bb, eye).reshape(SSM_WIDTH, SSM_FLAT)

    def bd_out(cc):
        return jnp.einsum("gcp,gh->hpgc", cc, eye).reshape(SSM_FLAT, SSM_WIDTH)

    bbd = jnp.concatenate([bd_in(bbar_re), bd_in(bbar_im)], axis=-1)
    cbd = jnp.concatenate([bd_out(p["ssm_c_re"][l]), -bd_out(p["ssm_c_im"][l])], axis=0)

    wr = jnp.zeros((D_MODEL, LANES), F32)
    wr = wr.at[:, 0:N_GROUPS].set(p["router_g_w"][l]).at[:, N_GROUPS:N_GROUPS + N_EXPERTS].set(p["router_e_w"][l])
    br = jnp.zeros((1, LANES), F32)
    br = br.at[0, 0:N_GROUPS].set(p["router_g_b"][l]).at[0, N_GROUPS:N_GROUPS + N_EXPERTS].set(p["router_e_b"][l])
    wr_hi = wr.astype(BF16)
    wr_lo = (wr - wr_hi.astype(F32)).astype(BF16)

    return {
        "norm1_g": p["norm1_g"][l][None, :], "norm2_g": p["norm2_g"][l][None, :],
        "w_in": w_in_p.astype(BF16),
        "gm_wcat": gm_wcat.astype(BF16), "gm_bt": gm_bt, "gm_w0": gm_w0, "gm_b0": gm_b0,
        "q_norm_g": p["q_norm_g"][l][None, :], "wq": wq.astype(BF16), "wuk_bd": wuk_bd.astype(BF16),
        "kv_norm_g": p["kv_norm_g"][l][None, :], "wuv_bd": wuv_bd.astype(BF16), "wuv_flat": wuv_flat.astype(BF16),
        "bbd": bbd.astype(BF16), "cbd": cbd.astype(BF16),
        "a_re": abar_re.reshape(1, SSM_FLAT), "a_im": abar_im.reshape(1, SSM_FLAT),
        "ssm_d": p["ssm_d"][l][None, :], "glu_w": p["glu_w"][l].astype(BF16), "glu_b": p["glu_b"][l][None, :],
        "w_out": p["w_out"][l].astype(BF16),
        "wr_hi": wr_hi, "wr_lo": wr_lo, "br": br,
        "wg": p["exp_w_gate"][l].astype(BF16), "wu": p["exp_w_up"][l].astype(BF16),
        "wd": p["exp_w_down"][l].astype(BF16),
    }


def _consts(tms):
    rows = S5_SEQS * S5_SUB
    r = jnp.arange(rows)
    src = (r % S5_SEQS) * S5_SUB + r // S5_SEQS
    perm = (jnp.arange(rows)[None, :] == src[:, None]).astype(BF16)
    ltri = {tm: (jnp.arange(tm)[None, :] < jnp.arange(tm)[:, None]).astype(BF16) for tm in tms}
    return {"perm": perm, "perm_t": jnp.transpose(perm), "ltri": ltri}


def kernel(x_prompt, x_sample, cache_ckv, cache_kpe, state_ssm_re, state_ssm_im, page_table, c_prompt, c_sample, norm1_g, norm2_g, ada_w, ada_b, w_in, sp_w, sp_b, ssm_lam_re, ssm_lam_im, ssm_b_re, ssm_b_im, ssm_c_re, ssm_c_im, ssm_d, ssm_log_step, glu_w, glu_b, q_norm_g, w_uq, kv_norm_g, w_uk, w_uv, w_out, router_g_w, router_g_b, router_e_w, router_e_b, exp_w_gate, exp_w_up, exp_w_down, final_norm_g):
    params = dict(norm1_g=norm1_g, norm2_g=norm2_g, w_in=w_in, sp_w=sp_w, sp_b=sp_b, ssm_lam_re=ssm_lam_re,
                  ssm_lam_im=ssm_lam_im, ssm_b_re=ssm_b_re, ssm_b_im=ssm_b_im, ssm_c_re=ssm_c_re, ssm_c_im=ssm_c_im,
                  ssm_d=ssm_d, ssm_log_step=ssm_log_step, glu_w=glu_w, glu_b=glu_b, q_norm_g=q_norm_g, w_uq=w_uq,
                  kv_norm_g=kv_norm_g, w_uk=w_uk, w_uv=w_uv, w_out=w_out, router_g_w=router_g_w,
                  router_g_b=router_g_b, router_e_w=router_e_w, router_e_b=router_e_b, exp_w_gate=exp_w_gate,
                  exp_w_up=exp_w_up, exp_w_down=exp_w_down)
    depth = w_in.shape[0]
    nb_p, seq, _ = x_prompt.shape
    nb_s = x_sample.shape[0]
    assert x_sample.shape[1] == 1 and nb_p % S5_SEQS == 0 and seq % CHUNK == 0
    past_len = page_table.shape[1] * cache_ckv.shape[2]
    tm_p = min(512, seq)
    tm_s = nb_s
    blk_p = min(512, seq)
    blk_s = 128
    consts = _consts({tm_p, tm_s})
    final_g = final_norm_g[None, :]

    mod = _modulation(jnp.concatenate([c_prompt, c_sample], axis=0), ada_w, ada_b)
    rope_p = _rope_tables(0, seq)
    rope_s = {k: jnp.broadcast_to(v, (nb_s, v.shape[1])) for k, v in _rope_tables(past_len, 1).items()}

    cache_kpet = jnp.swapaxes(cache_kpe, 2, 3)

    xp = x_prompt.reshape(nb_p * seq, D_MODEL)
    xs = x_sample.reshape(nb_s, D_MODEL)
    outs = {k: [] for k in ("p_ckv", "p_kpe", "p_sre", "p_sim", "s_ckv", "s_kpe", "s_sre", "s_sim", "s_v")}
    for l in range(depth):
        lw = _prep_layer(l, params)
        last = l == depth - 1
        mod_p = mod[l, :nb_p].reshape(nb_p, 1, 6 * D_MODEL)
        ya, bu, qlat, qpe, ckv, kpe, kcat, _ = _proj(xp, mod_p, lw, rope_p, seq=seq, tm=tm_p, chunked=True)
        yb, sre, sim = _s5_prompt(bu, lw, consts, batch=nb_p, seq=seq)
        yc = _attn_prompt(qlat, qpe, kcat, lw, batch=nb_p, seq=seq)
        x1, h2, route, counts = _post(xp, ya, yb, yc, mod_p, lw, consts, seq=seq, tm=tm_p, per_seq=True)
        xp = _moe(x1, h2, route, counts, mod_p, lw, final_g, seq=seq, tm=tm_p, per_seq=True, blk=blk_p, final=last)
        outs["p_ckv"].append(ckv.reshape(nb_p, seq, KV_RANK))
        outs["p_kpe"].append(kpe.reshape(nb_p, seq, QK_ROPE))
        outs["p_sre"].append(sre.reshape(nb_p, SSM_GROUPS, SSM_STATE))
        outs["p_sim"].append(sim.reshape(nb_p, SSM_GROUPS, SSM_STATE))

        mod_s = mod[l, nb_p:]
        ya, bu, qlat, qpe, ckv, kpe, kcat, av = _proj(xs, mod_s, lw, rope_s, seq=1, tm=tm_s, chunked=False)
        yb, sre, sim = _s5_step(bu, state_ssm_re[l].reshape(nb_s, SSM_FLAT), state_ssm_im[l].reshape(nb_s, SSM_FLAT),
                                lw)
        yc = _attn_sample(qlat, qpe, ckv, kpe, page_table, cache_ckv, cache_kpet, lw, layer=l)
        x1, h2, route, counts = _post(xs, ya, yb, yc, mod_s, lw, consts, seq=1, tm=tm_s, per_seq=False)
        xs = _moe(x1, h2, route, counts, mod_s, lw, final_g, seq=1, tm=tm_s, per_seq=False, blk=blk_s, final=last)
        outs["s_ckv"].append(ckv.reshape(nb_s, 1, KV_RANK))
        outs["s_kpe"].append(kpe.reshape(nb_s, 1, QK_ROPE))
        outs["s_sre"].append(sre.reshape(nb_s, SSM_GROUPS, SSM_STATE))
        outs["s_sim"].append(sim.reshape(nb_s, SSM_GROUPS, SSM_STATE))
        outs["s_v"].append(av.reshape(nb_s, 1, A_WIDTH))

    return (xp.reshape(nb_p, seq, D_MODEL), xs.reshape(nb_s, 1, D_MODEL),
            jnp.stack(outs["p_ckv"]), jnp.stack(outs["p_kpe"]), jnp.stack(outs["p_sre"]), jnp.stack(outs["p_sim"]),
            jnp.stack(outs["s_ckv"]), jnp.stack(outs["s_kpe"]), jnp.stack(outs["s_sre"]), jnp.stack(outs["s_sim"]),
            jnp.stack(outs["s_v"]))
```

```python
import functools
import math

import jax
import jax.numpy as jnp
from jax import lax
from jax.experimental import pallas as pl
from jax.experimental.pallas import tpu as pltpu

F32 = jnp.float32
BF16 = jnp.bfloat16

D_MODEL = 1024
RMS_EPS = 1e-6
A_HEADS = 4
A_HEAD_DIM = 64
A_WIDTH = 256
CHUNK = 128
SSM_CH = 16
SSM_WIDTH = 256
SSM_GROUPS = 16
SSM_STATE = 64
SSM_FLAT = SSM_GROUPS * SSM_STATE
MLA_HEADS = 8
QK_NOPE = 64
QK_ROPE = 32
V_DIM = 64
Q_RANK = 256
KV_RANK = 128
MLA_WIDTH = 512
ROPE_THETA = 10000.0
ATTN_SCALE = (QK_NOPE + QK_ROPE) ** -0.5
N_GROUPS = 4
EXPERTS_PER_GROUP = 8
N_EXPERTS = 32
D_EXPERT = 512
KCAT = 256
IN_PAD = 1408

LANES = 128
SUBLANES = 8
VMEM_LIMIT = 56 * 1024 * 1024

NEG = -1e30


def _cparams(*sem):
    return pltpu.CompilerParams(dimension_semantics=tuple(sem), vmem_limit_bytes=VMEM_LIMIT)


def _dot(a, b):
    return jnp.dot(a, b, preferred_element_type=F32)


def _dot_nt(a, b):
    return lax.dot_general(a, b, (((1,), (1,)), ((), ())), preferred_element_type=F32)


def _rms(x):
    return x * lax.rsqrt(jnp.mean(x * x, axis=-1, keepdims=True) + RMS_EPS)


def _mod_kernel(c_ref, w_ref, b_ref, o_ref):
    c = c_ref[...]
    a = (c * jax.nn.sigmoid(c)).astype(BF16)
    o_ref[...] = _dot(a, w_ref[...].astype(BF16)) + b_ref[...]


def _modulation(c_all, ada_w, ada_b):
    depth = ada_w.shape[0]
    nb = c_all.shape[0]
    n_out = ada_w.shape[2]
    tn = D_MODEL
    return pl.pallas_call(
        _mod_kernel,
        grid=(depth, n_out // tn),
        in_specs=[
            pl.BlockSpec((nb, D_MODEL), lambda l, n: (0, 0)),
            pl.BlockSpec((None, D_MODEL, tn), lambda l, n: (l, 0, n)),
            pl.BlockSpec((None, 1, tn), lambda l, n: (l, 0, n)),
        ],
        out_specs=pl.BlockSpec((None, nb, tn), lambda l, n: (l, 0, n)),
        out_shape=jax.ShapeDtypeStruct((depth, nb, n_out), F32),
        compiler_params=_cparams("arbitrary", "arbitrary"),
        name="ada_mod",
    )(c_all, ada_w, ada_b.reshape(depth, 1, n_out))


def _proj_kernel(x_ref, sh_ref, sc_ref, g1_ref, win_ref, gma_ref, gmb_ref, qg_ref, wq_ref, wuk_ref, kvg_ref,
                 cq_ref, sq_ref, ck_ref, sk_ref,
                 ya_ref, bu_ref, qlat_ref, qpe_ref, ckv_ref, kpe_ref, kcat_ref, av_ref, *, chunked, tm):
    x = x_ref[...]
    h = _rms(x) * g1_ref[...] * (1.0 + sc_ref[...]) + sh_ref[...]
    proj = _dot(h.astype(BF16), win_ref[...])
    a_u = proj[:, 0:256]
    a_v = proj[:, 256:512]
    bu_ref[...] = proj[:, 512:768]
    c_q = proj[:, 768:1024]
    c_kv = proj[:, 1024:1152]
    kp4 = proj[:, 1152:1280]
    kp4s = proj[:, 1280:1408]
    av_ref[...] = a_v

    if chunked:
        lane_head = lax.broadcasted_iota(jnp.int32, (CHUNK, A_WIDTH), 1) // A_HEAD_DIM
        for c in range(tm // CHUNK):
            vb = a_v[c * CHUNK:(c + 1) * CHUNK].astype(BF16)
            vstack = jnp.concatenate(
                [jnp.where(lane_head == hd, vb, jnp.zeros_like(vb)) for hd in range(A_HEADS)], axis=0)
            gate = _dot(gma_ref[...], vstack) + gmb_ref[...]
            ya_ref[c * CHUNK:(c + 1) * CHUNK, :] = (a_u[c * CHUNK:(c + 1) * CHUNK] * gate).astype(BF16)
    else:
        ya_ref[...] = (a_u * (gma_ref[...] * a_v + gmb_ref[...])).astype(BF16)

    cq = (_rms(c_q) * qg_ref[...]).astype(BF16)
    qall = _dot(cq, wq_ref[...])
    qpe_ref[...] = (qall[:, 512:768] * cq_ref[...] + qall[:, 768:1024] * sq_ref[...]).astype(BF16)
    for j in range(MLA_HEADS // 2):
        qn = qall[:, j * 128:(j + 1) * 128].astype(BF16)
        qlat_ref[:, j * 256:(j + 1) * 256] = _dot(qn, wuk_ref[j]).astype(BF16)
    ckv = _rms(c_kv) * kvg_ref[...]
    ckv_ref[...] = ckv
    kpe4 = kp4 * ck_ref[...] + kp4s * sk_ref[...]
    kpe_ref[...] = kpe4[:, 0:QK_ROPE]
    kcat_ref[:, 0:KV_RANK] = ckv.astype(BF16)
    kcat_ref[:, KV_RANK:KCAT] = kpe4.astype(BF16)


def _proj(x, mod3, lw, rope, *, seq, tm, chunked):
    T = x.shape[0]
    nt = T // tm
    if chunked:
        per = seq // tm

        def modspec(k):
            return pl.BlockSpec((None, 1, D_MODEL), lambda i: (i // per, 0, k))

        def ropespec(w):
            return pl.BlockSpec((tm, w), lambda i: (i % per, 0))
        gma, gmb = lw["gm_wcat"], lw["gm_bt"]
    else:
        def modspec(k):
            return pl.BlockSpec((tm, D_MODEL), lambda i: (i, k))

        def ropespec(w):
            return pl.BlockSpec((tm, w), lambda i: (i, 0))
        gma, gmb = lw["gm_w0"], lw["gm_b0"]

    def full(a):
        nd = a.ndim
        return pl.BlockSpec(a.shape, lambda i: (0,) * nd)

    def rows(w):
        return pl.BlockSpec((tm, w), lambda i: (i, 0))

    outs = [
        jax.ShapeDtypeStruct((T, A_WIDTH), BF16),
        jax.ShapeDtypeStruct((T, SSM_WIDTH), F32),
        jax.ShapeDtypeStruct((T, MLA_HEADS * KV_RANK), BF16),
        jax.ShapeDtypeStruct((T, MLA_HEADS * QK_ROPE), BF16),
        jax.ShapeDtypeStruct((T, KV_RANK), F32),
        jax.ShapeDtypeStruct((T, QK_ROPE), F32),
        jax.ShapeDtypeStruct((T, KCAT), BF16),
        jax.ShapeDtypeStruct((T, A_WIDTH), F32),
    ]
    return pl.pallas_call(
        functools.partial(_proj_kernel, chunked=chunked, tm=tm),
        grid=(nt,),
        in_specs=[rows(D_MODEL), modspec(0), modspec(1), full(lw["norm1_g"]), full(lw["w_in"]), full(gma), full(gmb),
                  full(lw["q_norm_g"]), full(lw["wq"]), full(lw["wuk_bd"]), full(lw["kv_norm_g"]),
                  ropespec(256), ropespec(256), ropespec(128), ropespec(128)],
        out_specs=[rows(o.shape[1]) for o in outs],
        out_shape=outs,
        compiler_params=_cparams("arbitrary"),
        name="proj",
    )(x, mod3, mod3, lw["norm1_g"], lw["w_in"], gma, gmb, lw["q_norm_g"], lw["wq"], lw["wuk_bd"], lw["kv_norm_g"],
      rope["cq"], rope["sq"], rope["ck"], rope["sk"])


def _gelu_glu(y, gw_ref, gb_ref):
    z = jax.nn.gelu(y)
    gate = jax.nn.sigmoid(_dot(z.astype(BF16), gw_ref[...]) + gb_ref[...])
    return z * gate


S5_SUB = 32
S5_SEQS = 8


def _s5_prompt_kernel(u_ref, p_ref, pt_ref, bbd_ref, ar_ref, ai_ref, cbd_ref, d_ref, gw_ref, gb_ref,
                      yb_ref, sre_ref, sim_ref, bu_s, xs_s, st_s, *, nsub):
    i = pl.program_id(1)

    @pl.when(i == 0)
    def _():
        st_s[...] = jnp.zeros_like(st_s)

    rows = S5_SEQS * S5_SUB
    ar = jnp.broadcast_to(ar_ref[...], (S5_SEQS, SSM_FLAT))
    ai = jnp.broadcast_to(ai_ref[...], (S5_SEQS, SSM_FLAT))
    for k in range(nsub):
        u = u_ref[:, k * S5_SUB:(k + 1) * S5_SUB, :].reshape(rows, SSM_WIDTH)
        hi = u.astype(BF16)
        r1 = u - hi.astype(F32)
        mid = r1.astype(BF16)
        lo = (r1 - mid.astype(F32)).astype(BF16)
        up_hi = _dot(p_ref[...], hi)
        up = up_hi + _dot(p_ref[...], mid) + _dot(p_ref[...], lo)
        bu_s[...] = _dot(up_hi.astype(BF16), bbd_ref[...])

        def step(t, x):
            r = pl.multiple_of(t * S5_SEQS, S5_SEQS)
            b = bu_s[pl.ds(r, S5_SEQS), :]
            xr = x[:, :SSM_FLAT]
            xi = x[:, SSM_FLAT:]
            nr = ar * xr - ai * xi + b[:, :SSM_FLAT]
            ni = ar * xi + ai * xr + b[:, SSM_FLAT:]
            xn = jnp.concatenate([nr, ni], axis=-1)
            xs_s[pl.ds(r, S5_SEQS), :] = xn
            return xn

        x = lax.fori_loop(0, S5_SUB, step, st_s[...], unroll=4)
        st_s[...] = x
        y = _dot(xs_s[...].astype(BF16), cbd_ref[...]) + d_ref[...] * up
        yb = _gelu_glu(y, gw_ref, gb_ref).astype(BF16)
        back = _dot(pt_ref[...], yb).astype(BF16)
        yb_ref[:, k * S5_SUB:(k + 1) * S5_SUB, :] = back.reshape(S5_SEQS, S5_SUB, SSM_WIDTH)

    @pl.when(i == pl.num_programs(1) - 1)
    def _():
        sre_ref[...] = st_s[:, :SSM_FLAT]
        sim_ref[...] = st_s[:, SSM_FLAT:]


def _s5_prompt(bu, lw, consts, *, batch, seq):
    tt = min(128, seq)
    nsub = tt // S5_SUB
    u3 = bu.reshape(batch, seq, SSM_WIDTH)

    def full(a):
        nd = a.ndim
        return pl.BlockSpec(a.shape, lambda j, i: (0,) * nd)

    rows = S5_SEQS * S5_SUB
    yb, sre, sim = pl.pallas_call(
        functools.partial(_s5_prompt_kernel, nsub=nsub),
        grid=(batch // S5_SEQS, seq // tt),
        in_specs=[pl.BlockSpec((S5_SEQS, tt, SSM_WIDTH), lambda j, i: (j, i, 0)),
                  full(consts["perm"]), full(consts["perm_t"]), full(lw["bbd"]), full(lw["a_re"]), full(lw["a_im"]),
                  full(lw["cbd"]), full(lw["ssm_d"]), full(lw["glu_w"]), full(lw["glu_b"])],
        out_specs=[pl.BlockSpec((S5_SEQS, tt, SSM_WIDTH), lambda j, i: (j, i, 0)),
                   pl.BlockSpec((S5_SEQS, SSM_FLAT), lambda j, i: (j, 0)),
                   pl.BlockSpec((S5_SEQS, SSM_FLAT), lambda j, i: (j, 0))],
        out_shape=[jax.ShapeDtypeStruct((batch, seq, SSM_WIDTH), BF16),
                   jax.ShapeDtypeStruct((batch, SSM_FLAT), F32),
                   jax.ShapeDtypeStruct((batch, SSM_FLAT), F32)],
        scratch_shapes=[pltpu.VMEM((rows, 2 * SSM_FLAT), F32), pltpu.VMEM((rows, 2 * SSM_FLAT), F32),
                        pltpu.VMEM((S5_SEQS, 2 * SSM_FLAT), F32)],
        compiler_params=_cparams("arbitrary", "arbitrary"),
        name="s5_prompt",
    )(u3, consts["perm"], consts["perm_t"], lw["bbd"], lw["a_re"], lw["a_im"], lw["cbd"], lw["ssm_d"],
      lw["glu_w"], lw["glu_b"])
    return yb.reshape(batch * seq, SSM_WIDTH), sre, sim


def _s5_step_kernel(u_ref, x0r_ref, x0i_ref, bbd_ref, ar_ref, ai_ref, cbd_ref, d_ref, gw_ref, gb_ref,
                    yb_ref, sre_ref, sim_ref):
    u = u_ref[...]
    bu = _dot(u.astype(BF16), bbd_ref[...])
    ar = ar_ref[...]
    ai = ai_ref[...]
    xr = x0r_ref[...]
    xi = x0i_ref[...]
    nr = ar * xr - ai * xi + bu[:, :SSM_FLAT]
    ni = ar * xi + ai * xr + bu[:, SSM_FLAT:]
    sre_ref[...] = nr
    sim_ref[...] = ni
    xcat = jnp.concatenate([nr, ni], axis=-1).astype(BF16)
    y = _dot(xcat, cbd_ref[...]) + d_ref[...] * u
    yb_ref[...] = _gelu_glu(y, gw_ref, gb_ref).astype(BF16)


def _s5_step(bu, x0r, x0i, lw):
    n = bu.shape[0]
    args = (bu, x0r, x0i, lw["bbd"], lw["a_re"], lw["a_im"], lw["cbd"], lw["ssm_d"], lw["glu_w"], lw["glu_b"])
    return pl.pallas_call(
        _s5_step_kernel,
        out_shape=[jax.ShapeDtypeStruct((n, SSM_WIDTH), BF16), jax.ShapeDtypeStruct((n, SSM_FLAT), F32),
                   jax.ShapeDtypeStruct((n, SSM_FLAT), F32)],
        compiler_params=pltpu.CompilerParams(vmem_limit_bytes=VMEM_LIMIT),
        name="s5_step",
    )(*args)


EXP2_SCALE = ATTN_SCALE * math.log2(math.e)


def _attn_prompt_kernel(qlat_ref, qpe_ref, k_ref, wuv_ref, yc_ref, qs_s, m_s, acc_s, *, tq):
    i = pl.program_id(1)
    nr = MLA_HEADS * tq
    lane_head = lax.broadcasted_iota(jnp.int32, (tq, LANES), 1) // QK_ROPE
    for hd in range(MLA_HEADS):
        qs_s[hd * tq:(hd + 1) * tq, 0:KV_RANK] = qlat_ref[:, hd * KV_RANK:(hd + 1) * KV_RANK]
        grp = qpe_ref[:, (hd // 4) * LANES:(hd // 4 + 1) * LANES]
        qs_s[hd * tq:(hd + 1) * tq, KV_RANK:KCAT] = jnp.where(lane_head == hd % 4, grp, jnp.zeros_like(grp))
    m_s[...] = jnp.full((nr, LANES), NEG, F32)
    acc_s[...] = jnp.zeros((nr, 2 * KV_RANK), F32)
    ones = jnp.ones((tq, KV_RANK), BF16)

    ncol = tq // LANES

    def tile(j, masked):
        k0 = pl.multiple_of(j * tq, tq)
        kt = k_ref[pl.ds(k0, tq), :]
        v = jnp.concatenate([kt[:, 0:KV_RANK], ones], axis=-1)
        for hd in range(MLA_HEADS):
            rs = slice(hd * tq, (hd + 1) * tq)
            s = _dot_nt(qs_s[rs, :], kt) * EXP2_SCALE
            if masked:
                causal = (lax.broadcasted_iota(jnp.int32, (tq, tq), 1)
                          <= lax.broadcasted_iota(jnp.int32, (tq, tq), 0))
                s = jnp.where(causal, s, NEG)
            parts = [s[:, c * LANES:(c + 1) * LANES] for c in range(ncol)]
            mx = parts[0]
            for part in parts[1:]:
                mx = jnp.maximum(mx, part)
            m_old = m_s[rs, :]
            m_new = jnp.maximum(m_old, jnp.max(mx, axis=-1, keepdims=True))
            alpha = jnp.exp2(m_old - m_new)
            m_s[rs, :] = m_new
            p = jnp.concatenate([jnp.exp2(part - m_new).astype(BF16) for part in parts], axis=-1)
            acc_s[rs, :] = jnp.concatenate([alpha, alpha], axis=-1) * acc_s[rs, :] + _dot(p, v)

    def off_diag(j, carry):
        tile(j, False)
        return carry

    lax.fori_loop(0, i, off_diag, 0)
    tile(i, True)
    o = acc_s[:, 0:KV_RANK] / acc_s[:, KV_RANK:2 * KV_RANK]
    for j in range(MLA_HEADS // 2):
        pair = jnp.concatenate([o[(2 * j) * tq:(2 * j + 1) * tq], o[(2 * j + 1) * tq:(2 * j + 2) * tq]], axis=-1)
        yc_ref[:, j * 128:(j + 1) * 128] = _dot(pair.astype(BF16), wuv_ref[j]).astype(BF16)


def _attn_prompt(qlat, qpe, kcat, lw, *, batch, seq):
    tq = min(256, seq)
    nr = MLA_HEADS * tq
    yc = pl.pallas_call(
        functools.partial(_attn_prompt_kernel, tq=tq),
        grid=(batch, seq // tq),
        in_specs=[pl.BlockSpec((None, tq, MLA_HEADS * KV_RANK), lambda b, i: (b, i, 0)),
                  pl.BlockSpec((None, tq, MLA_HEADS * QK_ROPE), lambda b, i: (b, i, 0)),
                  pl.BlockSpec((None, seq, KCAT), lambda b, i: (b, 0, 0)),
                  pl.BlockSpec(lw["wuv_bd"].shape, lambda b, i: (0, 0, 0))],
        out_specs=pl.BlockSpec((None, tq, MLA_WIDTH), lambda b, i: (b, i, 0)),
        out_shape=jax.ShapeDtypeStruct((batch, seq, MLA_WIDTH), BF16),
        scratch_shapes=[pltpu.VMEM((nr, KCAT), BF16), pltpu.VMEM((nr, LANES), F32),
                        pltpu.VMEM((nr, 2 * KV_RANK), F32)],
        compiler_params=_cparams("arbitrary", "arbitrary"),
        name="attn_prompt",
    )(qlat.reshape(batch, seq, -1), qpe.reshape(batch, seq, -1), kcat.reshape(batch, seq, KCAT), lw["wuv_bd"])
    return yc.reshape(batch * seq, MLA_WIDTH)


def _attn_sample_kernel(pt_ref, ql_ref, qp_ref, kn_ref, pn_ref, wuv_ref, ckv_hbm, kpet_hbm, yc_ref,
                        cbuf, pbuf, sem, m_s, l_s, acc_s, *, layer, pc, nchunk, page):
    b = pl.program_id(0)
    c = pl.program_id(1)
    step = b * nchunk + c
    nstep = pl.num_programs(0) * nchunk
    slot = step % 2

    def page_copies(bb, cc, sl, p):
        pg = pt_ref[bb, cc * pc + p]
        off = pl.multiple_of(p * page, page)
        return (pltpu.make_async_copy(ckv_hbm.at[layer, pg], cbuf.at[sl, p], sem.at[sl, 0]),
                pltpu.make_async_copy(kpet_hbm.at[layer, pg], pbuf.at[sl, :, pl.ds(off, page)], sem.at[sl, 1]))

    def start_chunk(bb, cc, sl):
        def body(p, carry):
            for cp in page_copies(bb, cc, sl, p):
                cp.start()
            return carry
        lax.fori_loop(0, pc, body, 0, unroll=min(4, pc))

    def wait_chunk(sl):
        pltpu.make_async_copy(ckv_hbm.at[layer, pl.ds(0, pc)], cbuf.at[sl], sem.at[sl, 0]).wait()
        pltpu.make_async_copy(pbuf.at[1 - sl], pbuf.at[sl], sem.at[sl, 1]).wait()

    @pl.when(step == 0)
    def _():
        start_chunk(b, c, slot)

    @pl.when(step + 1 < nstep)
    def _():
        nxt = step + 1
        start_chunk(nxt // nchunk, nxt % nchunk, 1 - slot)

    ql = ql_ref[...]
    qp = qp_ref[...]

    @pl.when(c == 0)
    def _():
        kn = kn_ref[...].astype(BF16).astype(F32)
        pn = pn_ref[...].astype(BF16).astype(F32)
        s_new = (jnp.sum(ql.astype(F32) * kn, axis=-1, keepdims=True)
                 + jnp.sum(qp.astype(F32) * pn, axis=-1, keepdims=True)) * ATTN_SCALE
        m_s[...] = s_new
        l_s[...] = jnp.ones_like(l_s)
        acc_s[...] = jnp.broadcast_to(kn, acc_s.shape)

    wait_chunk(slot)

    kc = cbuf[slot].reshape(pc * page, KV_RANK).astype(BF16)
    kpt = pbuf[slot].astype(BF16)
    s = (_dot_nt(ql, kc) + _dot(qp, kpt)) * ATTN_SCALE
    m_old = m_s[...]
    m_new = jnp.maximum(m_old, jnp.max(s, axis=-1, keepdims=True))
    alpha = jnp.exp(m_old - m_new)
    p = jnp.exp(s - m_new)
    l_s[...] = alpha * l_s[...] + jnp.sum(p, axis=-1, keepdims=True)
    acc_s[...] = alpha * acc_s[...] + _dot(p.astype(BF16), kc)
    m_s[...] = m_new

    @pl.when(c == nchunk - 1)
    def _():
        o = (acc_s[...] / l_s[...]).astype(BF16)
        yfull = _dot(o, wuv_ref[...])
        sel = lax.broadcasted_iota(jnp.int32, yfull.shape, 1) // V_DIM == lax.broadcasted_iota(
            jnp.int32, yfull.shape, 0)
        yc_ref[...] = jnp.sum(jnp.where(sel, yfull, 0.0), axis=0, keepdims=True).astype(BF16)


SAMPLE_CHUNK_ROWS = 16384


def _attn_sample(qlat, qpe, ckv_new, kpe_new, page_table, cache_ckv, cache_kpet, lw, *, layer):
    n, npages = page_table.shape
    page = cache_ckv.shape[2]
    assert page % LANES == 0
    pc = max(1, SAMPLE_CHUNK_ROWS // page)
    while npages % pc:
        pc -= 1
    nchunk = npages // pc
    ql = qlat.reshape(n, MLA_HEADS, KV_RANK)
    qp = qpe.reshape(n, MLA_HEADS, QK_ROPE)
    grid_spec = pltpu.PrefetchScalarGridSpec(
        num_scalar_prefetch=1,
        grid=(n, nchunk),
        in_specs=[pl.BlockSpec((None, MLA_HEADS, KV_RANK), lambda b, c, pt: (b, 0, 0)),
                  pl.BlockSpec((None, MLA_HEADS, QK_ROPE), lambda b, c, pt: (b, 0, 0)),
                  pl.BlockSpec((None, 1, KV_RANK), lambda b, c, pt: (b, 0, 0)),
                  pl.BlockSpec((None, 1, QK_ROPE), lambda b, c, pt: (b, 0, 0)),
                  pl.BlockSpec(lw["wuv_flat"].shape, lambda b, c, pt: (0, 0)),
                  pl.BlockSpec(memory_space=pl.ANY),
                  pl.BlockSpec(memory_space=pl.ANY)],
        out_specs=pl.BlockSpec((None, 1, MLA_WIDTH), lambda b, c, pt: (b, 0, 0)),
        scratch_shapes=[pltpu.VMEM((2, pc, page, KV_RANK), F32), pltpu.VMEM((2, QK_ROPE, pc * page), F32),
                        pltpu.SemaphoreType.DMA((2, 2)),
                        pltpu.VMEM((MLA_HEADS, 1), F32), pltpu.VMEM((MLA_HEADS, 1), F32),
                        pltpu.VMEM((MLA_HEADS, KV_RANK), F32)],
    )
    yc = pl.pallas_call(
        functools.partial(_attn_sample_kernel, layer=layer, pc=pc, nchunk=nchunk, page=page),
        grid_spec=grid_spec,
        out_shape=jax.ShapeDtypeStruct((n, 1, MLA_WIDTH), BF16),
        compiler_params=_cparams("arbitrary", "arbitrary"),
        name="attn_sample",
    )(page_table, ql, qp, ckv_new.reshape(n, 1, KV_RANK), kpe_new.reshape(n, 1, QK_ROPE), lw["wuv_flat"],
      cache_ckv, cache_kpet)
    return yc.reshape(n, MLA_WIDTH)


def _post_kernel(x_ref, ya_ref, yb_ref, yc_ref, wo_ref, g1_ref, n2_ref, sc2_ref, sh2_ref, wrhl_ref, br_ref,
                 ltri_ref, x1_ref, h2_ref, route_ref, routet_ref, cnt_ref, run_s):
    i = pl.program_id(0)

    @pl.when(i == 0)
    def _():
        run_s[...] = jnp.zeros_like(run_s)

    y = (_dot(ya_ref[...], wo_ref[0:256, :]) + _dot(yb_ref[...], wo_ref[256:512, :])
         + _dot(yc_ref[...], wo_ref[512:1024, :]))
    x1 = x_ref[...] + g1_ref[...] * y
    x1_ref[...] = x1
    h2 = _rms(x1) * n2_ref[...] * (1.0 + sc2_ref[...]) + sh2_ref[...]
    h2_ref[...] = _pack_rows(h2)

    hh = h2.astype(BF16)
    hl = (h2 - hh.astype(F32)).astype(BF16)
    d1 = _dot(hh, wrhl_ref[...])
    logits = d1[:, 0:LANES] + d1[:, LANES:2 * LANES] + _dot(hl, wrhl_ref[:, 0:LANES]) + br_ref[...]
    tm = logits.shape[0]
    lane = lax.broadcasted_iota(jnp.int32, (tm, LANES), 1).astype(F32)
    big = float(LANES)

    gl = jnp.where(lane < N_GROUPS, logits, NEG)
    gmax = jnp.max(gl, axis=-1, keepdims=True)
    gidx = jnp.min(jnp.where(gl == gmax, lane, big), axis=-1, keepdims=True)
    gden = jnp.sum(jnp.where(lane < N_GROUPS, jnp.exp(gl - gmax), 0.0), axis=-1, keepdims=True)
    gprob = 1.0 / gden

    lo = N_GROUPS + EXPERTS_PER_GROUP * gidx
    el = jnp.where((lane >= lo) & (lane < lo + EXPERTS_PER_GROUP), logits, NEG)
    m1 = jnp.max(el, axis=-1, keepdims=True)
    i1 = jnp.min(jnp.where(el == m1, lane, big), axis=-1, keepdims=True)
    el2 = jnp.where(lane == i1, NEG, el)
    m2 = jnp.max(el2, axis=-1, keepdims=True)
    i2 = jnp.min(jnp.where(el2 == m2, lane, big), axis=-1, keepdims=True)
    e21 = jnp.exp(m2 - m1)
    w1 = gprob / (1.0 + e21)
    w2 = gprob * e21 / (1.0 + e21)
    e1 = i1 - N_GROUPS
    e2 = i2 - N_GROUPS

    oh1 = lane == e1
    oh2 = lane == e2
    both = jnp.where(oh1, 1.0, 0.0) + jnp.where(oh2, 1.0, 0.0)
    tot = _dot(ltri_ref[...], both.astype(BF16)) + run_s[...]
    r1 = jnp.sum(jnp.where(oh1, tot, 0.0), axis=-1, keepdims=True)
    r2 = jnp.sum(jnp.where(oh2, tot, 0.0), axis=-1, keepdims=True)
    run = run_s[...] + jnp.sum(both, axis=0, keepdims=True)
    run_s[...] = run
    cnt_ref[...] = jnp.broadcast_to(run, cnt_ref.shape)

    route = jnp.where(lane == 0, e1, jnp.where(lane == 1, e2, jnp.where(lane == 2, r1, jnp.where(
        lane == 3, r2, jnp.where(lane == 4, w1, jnp.where(lane == 5, w2, 0.0))))))
    route_ref[...] = route[:, 0:SUBLANES]
    routet_ref[...] = jnp.transpose(route)[0:SUBLANES, :]


def _post(x, ya, yb, yc, mod3, lw, consts, *, seq, tm, per_seq):
    T = x.shape[0]
    nt = T // tm
    if per_seq:
        per = seq // tm

        def modspec(k):
            return pl.BlockSpec((None, 1, D_MODEL), lambda i: (i // per, 0, k))
    else:
        def modspec(k):
            return pl.BlockSpec((tm, D_MODEL), lambda i: (i, k))

    def full(a):
        nd = a.ndim
        return pl.BlockSpec(a.shape, lambda i: (0,) * nd)

    def rows(w):
        return pl.BlockSpec((tm, w), lambda i: (i, 0))

    ltri = consts["ltri"][tm]
    return pl.pallas_call(
        _post_kernel,
        grid=(nt,),
        in_specs=[rows(D_MODEL), rows(A_WIDTH), rows(SSM_WIDTH), rows(MLA_WIDTH), full(lw["w_out"]),
                  modspec(2), full(lw["norm2_g"]), modspec(4), modspec(3),
                  full(lw["wr_hl"]), full(lw["br"]), full(ltri)],
        out_specs=[rows(D_MODEL), rows(HALF), rows(SUBLANES), pl.BlockSpec((SUBLANES, tm), lambda i: (0, i)),
                   pl.BlockSpec((SUBLANES, LANES), lambda i: (0, 0))],
        out_shape=[jax.ShapeDtypeStruct((T, D_MODEL), F32), jax.ShapeDtypeStruct((T, HALF), jnp.uint32),
                   jax.ShapeDtypeStruct((T, SUBLANES), F32), jax.ShapeDtypeStruct((SUBLANES, T), F32),
                   jax.ShapeDtypeStruct((SUBLANES, LANES), F32)],
        scratch_shapes=[pltpu.VMEM((1, LANES), F32)],
        compiler_params=_cparams("arbitrary"),
        name="post",
    )(x, ya, yb, yc, lw["w_out"], mod3, lw["norm2_g"], mod3, mod3, lw["wr_hl"], lw["br"], ltri)


def _idx_copy(dest_hbm, idx_s, isem, step, sl):
    n = dest_hbm.shape[1]
    return pltpu.make_async_copy(dest_hbm.at[step], idx_s.at[pl.ds(pl.multiple_of(sl * n, n), n)], isem.at[sl])


def _stage_indices(dest_hbm, idx_s, isem):
    i = pl.program_id(0)
    n = pl.num_programs(0)
    slot = i % 2

    def cp(step, sl):
        return _idx_copy(dest_hbm, idx_s, isem, step, sl)

    @pl.when(i == 0)
    def _():
        cp(0, 0).start()

    cp(i, slot).wait()

    @pl.when(i + 1 < n)
    def _():
        cp(i + 1, 1 - slot).start()

    return slot


HALF = D_MODEL // 2
HI_MASK = 0xFFFF0000


def _pack_rows(x):
    u = lax.bitcast_convert_type(x.astype(BF16).astype(F32), jnp.uint32)
    return (u[:, :HALF] >> 16) | (u[:, HALF:] & jnp.uint32(HI_MASK))


def _unpack_rows(w):
    lo = lax.bitcast_convert_type(w << 16, F32)
    hi = lax.bitcast_convert_type(w & jnp.uint32(HI_MASK), F32)
    return lo, hi


def _dispatch_kernel(dest_hbm, h_ref, xin_hbm, xbuf_hbm, idx_s, isem, hs, rsem, *, tm, nt):
    del xin_hbm
    i = pl.program_id(0)
    slot = _stage_indices(dest_hbm, idx_s, isem)
    hs[slot] = h_ref[...]

    src = hs.at[slot]
    sem = rsem.at[slot]
    base = slot * (2 * tm)

    def issue(g, carry):
        r0 = pl.multiple_of(g * SUBLANES, SUBLANES)
        grp = src.at[pl.ds(r0, SUBLANES)]
        for k in range(SUBLANES):
            for half in range(2):
                d = idx_s[base + half * tm + r0 + k]
                pltpu.make_async_copy(grp.at[pl.ds(k, 1)], xbuf_hbm.at[pl.ds(d, 1)], sem).start(priority=k % 2)
        return carry

    lax.fori_loop(0, tm // SUBLANES, issue, 0)

    def drain(sl):
        for _ in range(2):
            pltpu.make_async_copy(hs.at[sl], xbuf_hbm.at[pl.ds(0, tm)], rsem.at[sl]).wait()

    @pl.when(i > 0)
    def _():
        drain(1 - slot)

    @pl.when(i == nt - 1)
    def _():
        drain(slot)


def _dispatch(h2p, dest2, n_rows, *, tm):
    T = h2p.shape[0]
    nt = T // tm
    xzero = jnp.zeros((n_rows, HALF), jnp.uint32)
    return pl.pallas_call(
        functools.partial(_dispatch_kernel, tm=tm, nt=nt),
        grid=(nt,),
        in_specs=[pl.BlockSpec(memory_space=pl.ANY), pl.BlockSpec((tm, HALF), lambda i: (i, 0)),
                  pl.BlockSpec(memory_space=pl.ANY)],
        out_specs=pl.BlockSpec(memory_space=pl.ANY),
        out_shape=jax.ShapeDtypeStruct((n_rows, HALF), jnp.uint32),
        scratch_shapes=[pltpu.SMEM((4 * tm,), jnp.int32), pltpu.SemaphoreType.DMA((2,)),
                        pltpu.VMEM((2, tm, HALF), jnp.uint32), pltpu.SemaphoreType.DMA((2,))],
        input_output_aliases={2: 0},
        compiler_params=_cparams("arbitrary"),
        name="moe_dispatch",
    )(dest2, h2p, xzero)


def _ffn_kernel(be_ref, nu_ref, x_ref, wg_ref, wu_ref, wd_ref, y_ref, wg_s, wu_s, wd_s):
    i = pl.program_id(0)

    @pl.when((i == 0) | (be_ref[i] != be_ref[jnp.maximum(i - 1, 0)]))
    def _():
        wg_s[...] = wg_ref[...].astype(BF16)
        wu_s[...] = wu_ref[...].astype(BF16)
        wd_s[...] = wd_ref[...].astype(BF16)

    @pl.when(i < nu_ref[0])
    def _():
        lo, hi = _unpack_rows(x_ref[...])
        x = jnp.concatenate([lo, hi], axis=-1).astype(BF16)
        g = _dot(x, wg_s[...])
        u = _dot(x, wu_s[...])
        mid = (g * jax.nn.sigmoid(g) * u).astype(BF16)
        y_ref[...] = _pack_rows(_dot(mid, wd_s[...]))

    @pl.when(i >= nu_ref[0])
    def _():
        y_ref[...] = jnp.zeros_like(y_ref)


def _ffn(xbuf, block_e, n_used, w_gate, w_up, w_down, *, layer, blk):
    n_rows = xbuf.shape[0]
    nb = n_rows // blk
    grid_spec = pltpu.PrefetchScalarGridSpec(
        num_scalar_prefetch=2,
        grid=(nb,),
        in_specs=[pl.BlockSpec((blk, HALF), lambda i, be, nu: (i, 0)),
                  pl.BlockSpec((None, None, D_MODEL, D_EXPERT), lambda i, be, nu: (layer, be[i], 0, 0)),
                  pl.BlockSpec((None, None, D_MODEL, D_EXPERT), lambda i, be, nu: (layer, be[i], 0, 0)),
                  pl.BlockSpec((None, None, D_EXPERT, D_MODEL), lambda i, be, nu: (layer, be[i], 0, 0))],
        out_specs=pl.BlockSpec((blk, HALF), lambda i, be, nu: (i, 0)),
        scratch_shapes=[pltpu.VMEM((D_MODEL, D_EXPERT), BF16), pltpu.VMEM((D_MODEL, D_EXPERT), BF16),
                        pltpu.VMEM((D_EXPERT, D_MODEL), BF16)],
    )
    return pl.pallas_call(
        _ffn_kernel,
        grid_spec=grid_spec,
        out_shape=jax.ShapeDtypeStruct((n_rows, HALF), jnp.uint32),
        compiler_params=_cparams("arbitrary"),
        name="moe_ffn",
    )(block_e, n_used, xbuf, w_gate, w_up, w_down)


def _combine_kernel(dest_hbm, x1_ref, g2_ref, route_ref, fg_ref, ybuf_hbm, out_ref, idx_s, isem, ysc, rsem,
                    *, tm, nt, final):
    i = pl.program_id(0)
    slot = i % 2

    def idx_copy(step, sl):
        return _idx_copy(dest_hbm, idx_s, isem, step, sl)

    def issue_tile(sl):
        sem = rsem.at[sl]
        base = sl * (2 * tm)

        def issue(g, carry):
            r0 = pl.multiple_of(g * SUBLANES, SUBLANES)
            for half in range(2):
                grp = ysc.at[sl, half, pl.ds(r0, SUBLANES)]
                for k in range(SUBLANES):
                    d = idx_s[base + half * tm + r0 + k]
                    pltpu.make_async_copy(ybuf_hbm.at[pl.ds(d, 1)], grp.at[pl.ds(k, 1)], sem).start(priority=k % 2)
            return carry

        lax.fori_loop(0, tm // SUBLANES, issue, 0)

    @pl.when(i == 0)
    def _():
        idx_copy(0, 0).start()
        idx_copy(0, 0).wait()
        issue_tile(0)
        if nt > 1:
            idx_copy(1, 1).start()

    if nt > 1:
        @pl.when(i + 1 < nt)
        def _():
            idx_copy(i + 1, 1 - slot).wait()
            issue_tile(1 - slot)

            @pl.when(i + 2 < nt)
            def _():
                idx_copy(i + 2, slot).start()

    for k in range(2):
        pltpu.make_async_copy(ybuf_hbm.at[pl.ds(0, tm)], ysc.at[slot, k], rsem.at[slot]).wait()

    route = route_ref[...]
    w1 = route[:, 4:5]
    w2 = route[:, 5:6]
    lo1, hi1 = _unpack_rows(ysc[slot, 0])
    lo2, hi2 = _unpack_rows(ysc[slot, 1])
    ff = jnp.concatenate([lo1 * w1 + lo2 * w2, hi1 * w1 + hi2 * w2], axis=-1)
    x2 = x1_ref[...] + g2_ref[...] * ff
    if final:
        x2 = _rms(x2) * fg_ref[...]
    out_ref[...] = x2


def _combine(x1, mod3, route, ybuf, dest2, final_g, *, seq, tm, per_seq, final):
    T = x1.shape[0]
    if per_seq:
        per = seq // tm
        g2spec = pl.BlockSpec((None, 1, D_MODEL), lambda i: (i // per, 0, 5))
    else:
        g2spec = pl.BlockSpec((tm, D_MODEL), lambda i: (i, 5))
    return pl.pallas_call(
        functools.partial(_combine_kernel, tm=tm, nt=T // tm, final=final),
        grid=(T // tm,),
        in_specs=[pl.BlockSpec(memory_space=pl.ANY), pl.BlockSpec((tm, D_MODEL), lambda i: (i, 0)), g2spec,
                  pl.BlockSpec((tm, SUBLANES), lambda i: (i, 0)), pl.BlockSpec((1, D_MODEL), lambda i: (0, 0)),
                  pl.BlockSpec(memory_space=pl.ANY)],
        out_specs=pl.BlockSpec((tm, D_MODEL), lambda i: (i, 0)),
        out_shape=jax.ShapeDtypeStruct((T, D_MODEL), F32),
        scratch_shapes=[pltpu.SMEM((4 * tm,), jnp.int32), pltpu.SemaphoreType.DMA((2,)),
                        pltpu.VMEM((2, 2, tm, HALF), jnp.uint32), pltpu.SemaphoreType.DMA((2,))],
        compiler_params=_cparams("arbitrary"),
        name="moe_combine",
    )(dest2, x1, mod3, route, final_g, ybuf)


def _moe(x1, h2, route, route_t, counts, mod3, experts, final_g, *, layer, seq, tm, per_seq, blk, final):
    T = x1.shape[0]
    cnt = counts[0, :N_EXPERTS].astype(jnp.int32)
    padded = ((cnt + blk - 1) // blk) * blk
    pend = jnp.cumsum(padded)
    pstart = pend - padded
    nb = (2 * T + blk - 1) // blk + N_EXPERTS
    eid = route_t[0:2].astype(jnp.int32)
    rank = route_t[2:4].astype(jnp.int32)
    dest = pstart[eid] + rank
    dest2 = jnp.transpose(dest.reshape(2, T // tm, tm), (1, 0, 2)).reshape(T // tm, 2 * tm)
    starts = jnp.arange(nb, dtype=jnp.int32) * blk
    block_e = jnp.minimum(jnp.sum((pend[None, :] <= starts[:, None]).astype(jnp.int32), axis=1), N_EXPERTS - 1)
    n_used = (pend[-1:] // blk).astype(jnp.int32)
    xbuf = _dispatch(h2, dest2, nb * blk, tm=tm)
    ybuf = _ffn(xbuf, block_e, n_used, *experts, layer=layer, blk=blk)
    return _combine(x1, mod3, route, ybuf, dest2, final_g, seq=seq, tm=tm, per_seq=per_seq, final=final)


def _rope_tables(pos0, n):
    pos = (pos0 + jnp.arange(n)).astype(F32)
    inv = jnp.power(ROPE_THETA, -jnp.arange(0, QK_ROPE, 2, dtype=F32) / QK_ROPE)
    ang = pos[:, None] * inv[None, :]
    cos, sin = jnp.cos(ang), jnp.sin(ang)
    c32 = jnp.concatenate([cos, cos], axis=-1)
    s32 = jnp.concatenate([-sin, sin], axis=-1)
    return {"cq": jnp.tile(c32, (1, MLA_HEADS)), "sq": jnp.tile(s32, (1, MLA_HEADS)),
            "ck": jnp.tile(c32, (1, 4)), "sk": jnp.tile(s32, (1, 4))}


def _pair_blockdiag(w):
    z = jnp.zeros_like(w[:, 0])
    top = jnp.concatenate([w[:, 0], z], axis=-1)
    bot = jnp.concatenate([z, w[:, 1]], axis=-1)
    return jnp.concatenate([top, bot], axis=-2)


def _prep_layer(l, p):
    swap = jnp.concatenate([jnp.arange(16, 32), jnp.arange(0, 16)])
    w_in = p["w_in"][l]
    o5 = 2 * A_WIDTH + SSM_WIDTH + Q_RANK + KV_RANK
    kpe_w = w_in[:, o5:o5 + QK_ROPE]
    w_in_p = jnp.concatenate([w_in[:, :o5], jnp.tile(kpe_w, (1, 4)), jnp.tile(kpe_w[:, swap], (1, 4))], axis=-1)

    sp_w = jnp.tril(p["sp_w"][l])
    gm_wcat = jnp.transpose(sp_w, (1, 0, 2)).reshape(CHUNK, A_HEADS * CHUNK)
    gm_bt = jnp.repeat(jnp.transpose(p["sp_b"][l]), A_HEAD_DIM, axis=1)
    gm_w0 = jnp.repeat(p["sp_w"][l][:, 0, 0], A_HEAD_DIM)[None, :]
    gm_b0 = jnp.repeat(p["sp_b"][l][:, 0], A_HEAD_DIM)[None, :]

    w_uq = p["w_uq"][l].reshape(Q_RANK, MLA_HEADS, QK_NOPE + QK_ROPE)
    pe = w_uq[:, :, QK_NOPE:]
    wq = jnp.concatenate([w_uq[:, :, :QK_NOPE].reshape(Q_RANK, -1), pe.reshape(Q_RANK, -1),
                          pe[:, :, swap].reshape(Q_RANK, -1)], axis=-1)
    w_uk = jnp.transpose(p["w_uk"][l], (1, 2, 0))
    wuk_bd = _pair_blockdiag(w_uk.reshape(MLA_HEADS // 2, 2, QK_NOPE, KV_RANK))
    w_uv = jnp.transpose(p["w_uv"][l], (1, 0, 2))
    wuv_bd = _pair_blockdiag(w_uv.reshape(MLA_HEADS // 2, 2, KV_RANK, V_DIM))
    wuv_flat = p["w_uv"][l].reshape(KV_RANK, MLA_HEADS * V_DIM)

    dt = jnp.exp(p["ssm_log_step"][l])[:, None]
    lam_re, lam_im = p["ssm_lam_re"][l], p["ssm_lam_im"][l]
    mag = jnp.exp(lam_re * dt)
    abar_re = mag * jnp.cos(lam_im * dt)
    abar_im = mag * jnp.sin(lam_im * dt)
    den = lam_re * lam_re + lam_im * lam_im
    nr = abar_re - 1.0
    fr = (nr * lam_re + abar_im * lam_im) / den
    fi = (abar_im * lam_re - nr * lam_im) / den
    b_re, b_im = p["ssm_b_re"][l], p["ssm_b_im"][l]
    bbar_re = fr[..., None] * b_re - fi[..., None] * b_im
    bbar_im = fr[..., None] * b_im + fi[..., None] * b_re
    eye = jnp.eye(SSM_GROUPS, dtype=F32)

    def bd_in(bb):
        return jnp.einsum("gpc,gh->gchp", bb, eye).reshape(SSM_WIDTH, SSM_FLAT)

    def bd_out(cc):
        return jnp.einsum("gcp,gh->hpgc", cc, eye).reshape(SSM_FLAT, SSM_WIDTH)

    bbd = jnp.concatenate([bd_in(bbar_re), bd_in(bbar_im)], axis=-1)
    cbd = jnp.concatenate([bd_out(p["ssm_c_re"][l]), -bd_out(p["ssm_c_im"][l])], axis=0)

    wr = jnp.zeros((D_MODEL, LANES), F32)
    wr = wr.at[:, 0:N_GROUPS].set(p["router_g_w"][l]).at[:, N_GROUPS:N_GROUPS + N_EXPERTS].set(p["router_e_w"][l])
    br = jnp.zeros((1, LANES), F32)
    br = br.at[0, 0:N_GROUPS].set(p["router_g_b"][l]).at[0, N_GROUPS:N_GROUPS + N_EXPERTS].set(p["router_e_b"][l])
    wr_hi = wr.astype(BF16)
    wr_lo = (wr - wr_hi.astype(F32)).astype(BF16)

    return {
        "norm1_g": p["norm1_g"][l][None, :], "norm2_g": p["norm2_g"][l][None, :],
        "w_in": w_in_p.astype(BF16),
        "gm_wcat": gm_wcat.astype(BF16), "gm_bt": gm_bt, "gm_w0": gm_w0, "gm_b0": gm_b0,
        "q_norm_g": p["q_norm_g"][l][None, :], "wq": wq.astype(BF16), "wuk_bd": wuk_bd.astype(BF16),
        "kv_norm_g": p["kv_norm_g"][l][None, :], "wuv_bd": wuv_bd.astype(BF16), "wuv_flat": wuv_flat.astype(BF16),
        "bbd": bbd.astype(BF16), "cbd": cbd.astype(BF16),
        "a_re": abar_re.reshape(1, SSM_FLAT), "a_im": abar_im.reshape(1, SSM_FLAT),
        "ssm_d": p["ssm_d"][l][None, :], "glu_w": p["glu_w"][l].astype(BF16), "glu_b": p["glu_b"][l][None, :],
        "w_out": p["w_out"][l].astype(BF16),
        "wr_hl": jnp.concatenate([wr_hi, wr_lo], axis=-1), "br": br,
    }


def _consts(tms):
    rows = S5_SEQS * S5_SUB
    r = jnp.arange(rows)
    src = (r % S5_SEQS) * S5_SUB + r // S5_SEQS
    perm = (jnp.arange(rows)[None, :] == src[:, None]).astype(BF16)
    ltri = {tm: (jnp.arange(tm)[None, :] < jnp.arange(tm)[:, None]).astype(BF16) for tm in tms}
    return {"perm": perm, "perm_t": jnp.transpose(perm), "ltri": ltri}


def kernel(x_prompt, x_sample, cache_ckv, cache_kpe, state_ssm_re, state_ssm_im, page_table, c_prompt, c_sample, norm1_g, norm2_g, ada_w, ada_b, w_in, sp_w, sp_b, ssm_lam_re, ssm_lam_im, ssm_b_re, ssm_b_im, ssm_c_re, ssm_c_im, ssm_d, ssm_log_step, glu_w, glu_b, q_norm_g, w_uq, kv_norm_g, w_uk, w_uv, w_out, router_g_w, router_g_b, router_e_w, router_e_b, exp_w_gate, exp_w_up, exp_w_down, final_norm_g):
    params = dict(norm1_g=norm1_g, norm2_g=norm2_g, w_in=w_in, sp_w=sp_w, sp_b=sp_b, ssm_lam_re=ssm_lam_re,
                  ssm_lam_im=ssm_lam_im, ssm_b_re=ssm_b_re, ssm_b_im=ssm_b_im, ssm_c_re=ssm_c_re, ssm_c_im=ssm_c_im,
                  ssm_d=ssm_d, ssm_log_step=ssm_log_step, glu_w=glu_w, glu_b=glu_b, q_norm_g=q_norm_g, w_uq=w_uq,
                  kv_norm_g=kv_norm_g, w_uk=w_uk, w_uv=w_uv, w_out=w_out, router_g_w=router_g_w,
                  router_g_b=router_g_b, router_e_w=router_e_w, router_e_b=router_e_b, exp_w_gate=exp_w_gate,
                  exp_w_up=exp_w_up, exp_w_down=exp_w_down)
    depth = w_in.shape[0]
    nb_p, seq, _ = x_prompt.shape
    nb_s = x_sample.shape[0]
    assert x_sample.shape[1] == 1 and nb_p % S5_SEQS == 0 and seq % CHUNK == 0
    past_len = page_table.shape[1] * cache_ckv.shape[2]
    tm_p = min(512, seq)
    tm_s = nb_s
    blk_p = min(512, seq)
    blk_s = 128
    consts = _consts({tm_p, tm_s})
    final_g = final_norm_g[None, :]

    mod = _modulation(jnp.concatenate([c_prompt, c_sample], axis=0), ada_w, ada_b)
    rope_p = _rope_tables(0, seq)
    rope_s = {k: jnp.broadcast_to(v, (nb_s, v.shape[1])) for k, v in _rope_tables(past_len, 1).items()}

    cache_kpet = jnp.swapaxes(cache_kpe, 2, 3)
    experts = (exp_w_gate, exp_w_up, exp_w_down)

    xp = x_prompt.reshape(nb_p * seq, D_MODEL)
    xs = x_sample.reshape(nb_s, D_MODEL)
    outs = {k: [] for k in ("p_ckv", "p_kpe", "p_sre", "p_sim", "s_ckv", "s_kpe", "s_sre", "s_sim", "s_v")}
    for l in range(depth):
        lw = _prep_layer(l, params)
        last = l == depth - 1
        mod_p = mod[l, :nb_p].reshape(nb_p, 1, 6 * D_MODEL)
        ya, bu, qlat, qpe, ckv, kpe, kcat, _ = _proj(xp, mod_p, lw, rope_p, seq=seq, tm=tm_p, chunked=True)
        yb, sre, sim = _s5_prompt(bu, lw, consts, batch=nb_p, seq=seq)
        yc = _attn_prompt(qlat, qpe, kcat, lw, batch=nb_p, seq=seq)
        x1, h2, route, route_t, counts = _post(xp, ya, yb, yc, mod_p, lw, consts, seq=seq, tm=tm_p, per_seq=True)
        xp = _moe(x1, h2, route, route_t, counts, mod_p, experts, final_g, layer=l, seq=seq, tm=tm_p, per_seq=True,
                  blk=blk_p, final=last)
        outs["p_ckv"].append(ckv.reshape(nb_p, seq, KV_RANK))
        outs["p_kpe"].append(kpe.reshape(nb_p, seq, QK_ROPE))
        outs["p_sre"].append(sre.reshape(nb_p, SSM_GROUPS, SSM_STATE))
        outs["p_sim"].append(sim.reshape(nb_p, SSM_GROUPS, SSM_STATE))

        mod_s = mod[l, nb_p:]
        ya, bu, qlat, qpe, ckv, kpe, kcat, av = _proj(xs, mod_s, lw, rope_s, seq=1, tm=tm_s, chunked=False)
        yb, sre, sim = _s5_step(bu, state_ssm_re[l].reshape(nb_s, SSM_FLAT), state_ssm_im[l].reshape(nb_s, SSM_FLAT),
                                lw)
        yc = _attn_sample(qlat, qpe, ckv, kpe, page_table, cache_ckv, cache_kpet, lw, layer=l)
        x1, h2, route, route_t, counts = _post(xs, ya, yb, yc, mod_s, lw, consts, seq=1, tm=tm_s, per_seq=False)
        xs = _moe(x1, h2, route, route_t, counts, mod_s, experts, final_g, layer=l, seq=1, tm=tm_s, per_seq=False,
                  blk=blk_s, final=last)
        outs["s_ckv"].append(ckv.reshape(nb_s, 1, KV_RANK))
        outs["s_kpe"].append(kpe.reshape(nb_s, 1, QK_ROPE))
        outs["s_sre"].append(sre.reshape(nb_s, SSM_GROUPS, SSM_STATE))
        outs["s_sim"].append(sim.reshape(nb_s, SSM_GROUPS, SSM_STATE))
        outs["s_v"].append(av.reshape(nb_s, 1, A_WIDTH))

    return (xp.reshape(nb_p, seq, D_MODEL), xs.reshape(nb_s, 1, D_MODEL),
            jnp.stack(outs["p_ckv"]), jnp.stack(outs["p_kpe"]), jnp.stack(outs["p_sre"]), jnp.stack(outs["p_sim"]),
            jnp.stack(outs["s_ckv"]), jnp.stack(outs["s_kpe"]), jnp.stack(outs["s_sre"]), jnp.stack(outs["s_sim"]),
            jnp.stack(outs["s_v"]))
```

```python
import functools
import math

import jax
import jax.numpy as jnp
from jax import lax
from jax.experimental import pallas as pl
from jax.experimental.pallas import tpu as pltpu

F32 = jnp.float32
BF16 = jnp.bfloat16

D_MODEL = 1024
RMS_EPS = 1e-6
A_HEADS = 4
A_HEAD_DIM = 64
A_WIDTH = 256
CHUNK = 128
SSM_CH = 16
SSM_WIDTH = 256
SSM_GROUPS = 16
SSM_STATE = 64
SSM_FLAT = SSM_GROUPS * SSM_STATE
MLA_HEADS = 8
QK_NOPE = 64
QK_ROPE = 32
V_DIM = 64
Q_RANK = 256
KV_RANK = 128
MLA_WIDTH = 512
ROPE_THETA = 10000.0
ATTN_SCALE = (QK_NOPE + QK_ROPE) ** -0.5
N_GROUPS = 4
EXPERTS_PER_GROUP = 8
N_EXPERTS = 32
D_EXPERT = 512
KCAT = 256
IN_PAD = 1408

LANES = 128
SUBLANES = 8
VMEM_LIMIT = 56 * 1024 * 1024

NEG = -1e30


def _cparams(*sem):
    return pltpu.CompilerParams(dimension_semantics=tuple(sem), vmem_limit_bytes=VMEM_LIMIT)


def _dot(a, b):
    return jnp.dot(a, b, preferred_element_type=F32)


def _dot_nt(a, b):
    return lax.dot_general(a, b, (((1,), (1,)), ((), ())), preferred_element_type=F32)


def _rms(x):
    return x * lax.rsqrt(jnp.mean(x * x, axis=-1, keepdims=True) + RMS_EPS)


def _mod_kernel(c_ref, w_ref, b_ref, o_ref):
    c = c_ref[...]
    a = (c * jax.nn.sigmoid(c)).astype(BF16)
    o_ref[...] = _dot(a, w_ref[...].astype(BF16)) + b_ref[...]


def _modulation(c_all, ada_w, ada_b):
    depth = ada_w.shape[0]
    nb = c_all.shape[0]
    n_out = ada_w.shape[2]
    tn = D_MODEL
    return pl.pallas_call(
        _mod_kernel,
        grid=(depth, n_out // tn),
        in_specs=[
            pl.BlockSpec((nb, D_MODEL), lambda l, n: (0, 0)),
            pl.BlockSpec((None, D_MODEL, tn), lambda l, n: (l, 0, n)),
            pl.BlockSpec((None, 1, tn), lambda l, n: (l, 0, n)),
        ],
        out_specs=pl.BlockSpec((None, nb, tn), lambda l, n: (l, 0, n)),
        out_shape=jax.ShapeDtypeStruct((depth, nb, n_out), F32),
        compiler_params=_cparams("arbitrary", "arbitrary"),
        name="ada_mod",
    )(c_all, ada_w, ada_b.reshape(depth, 1, n_out))


def _proj_kernel(x_ref, sh_ref, sc_ref, g1_ref, win_ref, gma_ref, gmb_ref, qg_ref, wq_ref, wuk_ref, kvg_ref,
                 cq_ref, sq_ref, ck_ref, sk_ref,
                 ya_ref, bu_ref, qlat_ref, qpe_ref, ckv_ref, kpe_ref, kcat_ref, av_ref, *, chunked, tm):
    x = x_ref[...]
    h = _rms(x) * g1_ref[...] * (1.0 + sc_ref[...]) + sh_ref[...]
    proj = _dot(h.astype(BF16), win_ref[...])
    a_u = proj[:, 0:256]
    a_v = proj[:, 256:512]
    bu_ref[...] = proj[:, 512:768]
    c_q = proj[:, 768:1024]
    c_kv = proj[:, 1024:1152]
    kp4 = proj[:, 1152:1280]
    kp4s = proj[:, 1280:1408]
    av_ref[...] = a_v

    if chunked:
        lane_head = lax.broadcasted_iota(jnp.int32, (CHUNK, A_WIDTH), 1) // A_HEAD_DIM
        for c in range(tm // CHUNK):
            vb = a_v[c * CHUNK:(c + 1) * CHUNK].astype(BF16)
            vstack = jnp.concatenate(
                [jnp.where(lane_head == hd, vb, jnp.zeros_like(vb)) for hd in range(A_HEADS)], axis=0)
            gate = _dot(gma_ref[...], vstack) + gmb_ref[...]
            ya_ref[c * CHUNK:(c + 1) * CHUNK, :] = (a_u[c * CHUNK:(c + 1) * CHUNK] * gate).astype(BF16)
    else:
        ya_ref[...] = (a_u * (gma_ref[...] * a_v + gmb_ref[...])).astype(BF16)

    cq = (_rms(c_q) * qg_ref[...]).astype(BF16)
    qall = _dot(cq, wq_ref[...])
    qpe_ref[...] = (qall[:, 512:768] * cq_ref[...] + qall[:, 768:1024] * sq_ref[...]).astype(BF16)
    for j in range(MLA_HEADS // 2):
        qn = qall[:, j * 128:(j + 1) * 128].astype(BF16)
        qlat_ref[:, j * 256:(j + 1) * 256] = _dot(qn, wuk_ref[j]).astype(BF16)
    ckv = _rms(c_kv) * kvg_ref[...]
    ckv_ref[...] = ckv
    kpe4 = kp4 * ck_ref[...] + kp4s * sk_ref[...]
    kpe_ref[...] = kpe4[:, 0:QK_ROPE]
    kcat_ref[:, 0:KV_RANK] = ckv.astype(BF16)
    kcat_ref[:, KV_RANK:KCAT] = kpe4.astype(BF16)


def _proj(x, mod3, lw, rope, *, seq, tm, chunked):
    T = x.shape[0]
    nt = T // tm
    if chunked:
        per = seq // tm

        def modspec(k):
            return pl.BlockSpec((None, 1, D_MODEL), lambda i: (i // per, 0, k))

        def ropespec(w):
            return pl.BlockSpec((tm, w), lambda i: (i % per, 0))
        gma, gmb = lw["gm_wcat"], lw["gm_bt"]
    else:
        def modspec(k):
            return pl.BlockSpec((tm, D_MODEL), lambda i: (i, k))

        def ropespec(w):
            return pl.BlockSpec((tm, w), lambda i: (i, 0))
        gma, gmb = lw["gm_w0"], lw["gm_b0"]

    def full(a):
        nd = a.ndim
        return pl.BlockSpec(a.shape, lambda i: (0,) * nd)

    def rows(w):
        return pl.BlockSpec((tm, w), lambda i: (i, 0))

    outs = [
        jax.ShapeDtypeStruct((T, A_WIDTH), BF16),
        jax.ShapeDtypeStruct((T, SSM_WIDTH), F32),
        jax.ShapeDtypeStruct((T, MLA_HEADS * KV_RANK), BF16),
        jax.ShapeDtypeStruct((T, MLA_HEADS * QK_ROPE), BF16),
        jax.ShapeDtypeStruct((T, KV_RANK), F32),
        jax.ShapeDtypeStruct((T, QK_ROPE), F32),
        jax.ShapeDtypeStruct((T, KCAT), BF16),
        jax.ShapeDtypeStruct((T, A_WIDTH), F32),
    ]
    return pl.pallas_call(
        functools.partial(_proj_kernel, chunked=chunked, tm=tm),
        grid=(nt,),
        in_specs=[rows(D_MODEL), modspec(0), modspec(1), full(lw["norm1_g"]), full(lw["w_in"]), full(gma), full(gmb),
                  full(lw["q_norm_g"]), full(lw["wq"]), full(lw["wuk_bd"]), full(lw["kv_norm_g"]),
                  ropespec(256), ropespec(256), ropespec(128), ropespec(128)],
        out_specs=[rows(o.shape[1]) for o in outs],
        out_shape=outs,
        compiler_params=_cparams("arbitrary"),
        name="proj",
    )(x, mod3, mod3, lw["norm1_g"], lw["w_in"], gma, gmb, lw["q_norm_g"], lw["wq"], lw["wuk_bd"], lw["kv_norm_g"],
      rope["cq"], rope["sq"], rope["ck"], rope["sk"])


def _gelu_glu(y, gw_ref, gb_ref):
    z = jax.nn.gelu(y)
    gate = jax.nn.sigmoid(_dot(z.astype(BF16), gw_ref[...]) + gb_ref[...])
    return z * gate


S5_SUB = 32
S5_SEQS = 8


def _s5_prompt_kernel(u_ref, p_ref, pt_ref, bbd_ref, ar_ref, ai_ref, cbd_ref, d_ref, gw_ref, gb_ref,
                      yb_ref, sre_ref, sim_ref, bu_s, xs_s, st_s, *, nsub):
    i = pl.program_id(1)

    @pl.when(i == 0)
    def _():
        st_s[...] = jnp.zeros_like(st_s)

    rows = S5_SEQS * S5_SUB
    ar = jnp.broadcast_to(ar_ref[...], (S5_SEQS, SSM_FLAT))
    ai = jnp.broadcast_to(ai_ref[...], (S5_SEQS, SSM_FLAT))
    for k in range(nsub):
        u = u_ref[:, k * S5_SUB:(k + 1) * S5_SUB, :].reshape(rows, SSM_WIDTH)
        hi = u.astype(BF16)
        r1 = u - hi.astype(F32)
        mid = r1.astype(BF16)
        lo = (r1 - mid.astype(F32)).astype(BF16)
        up_hi = _dot(p_ref[...], hi)
        up = up_hi + _dot(p_ref[...], mid) + _dot(p_ref[...], lo)
        bu_s[...] = _dot(up_hi.astype(BF16), bbd_ref[...])

        def step(t, x):
            r = pl.multiple_of(t * S5_SEQS, S5_SEQS)
            b = bu_s[pl.ds(r, S5_SEQS), :]
            xr = x[:, :SSM_FLAT]
            xi = x[:, SSM_FLAT:]
            nr = ar * xr - ai * xi + b[:, :SSM_FLAT]
            ni = ar * xi + ai * xr + b[:, SSM_FLAT:]
            xn = jnp.concatenate([nr, ni], axis=-1)
            xs_s[pl.ds(r, S5_SEQS), :] = xn
            return xn

        x = lax.fori_loop(0, S5_SUB, step, st_s[...], unroll=4)
        st_s[...] = x
        y = _dot(xs_s[...].astype(BF16), cbd_ref[...]) + d_ref[...] * up
        yb = _gelu_glu(y, gw_ref, gb_ref).astype(BF16)
        back = _dot(pt_ref[...], yb).astype(BF16)
        yb_ref[:, k * S5_SUB:(k + 1) * S5_SUB, :] = back.reshape(S5_SEQS, S5_SUB, SSM_WIDTH)

    @pl.when(i == pl.num_programs(1) - 1)
    def _():
        sre_ref[...] = st_s[:, :SSM_FLAT]
        sim_ref[...] = st_s[:, SSM_FLAT:]


def _s5_prompt(bu, lw, consts, *, batch, seq):
    tt = min(128, seq)
    nsub = tt // S5_SUB
    u3 = bu.reshape(batch, seq, SSM_WIDTH)

    def full(a):
        nd = a.ndim
        return pl.BlockSpec(a.shape, lambda j, i: (0,) * nd)

    rows = S5_SEQS * S5_SUB
    yb, sre, sim = pl.pallas_call(
        functools.partial(_s5_prompt_kernel, nsub=nsub),
        grid=(batch // S5_SEQS, seq // tt),
        in_specs=[pl.BlockSpec((S5_SEQS, tt, SSM_WIDTH), lambda j, i: (j, i, 0)),
                  full(consts["perm"]), full(consts["perm_t"]), full(lw["bbd"]), full(lw["a_re"]), full(lw["a_im"]),
                  full(lw["cbd"]), full(lw["ssm_d"]), full(lw["glu_w"]), full(lw["glu_b"])],
        out_specs=[pl.BlockSpec((S5_SEQS, tt, SSM_WIDTH), lambda j, i: (j, i, 0)),
                   pl.BlockSpec((S5_SEQS, SSM_FLAT), lambda j, i: (j, 0)),
                   pl.BlockSpec((S5_SEQS, SSM_FLAT), lambda j, i: (j, 0))],
        out_shape=[jax.ShapeDtypeStruct((batch, seq, SSM_WIDTH), BF16),
                   jax.ShapeDtypeStruct((batch, SSM_FLAT), F32),
                   jax.ShapeDtypeStruct((batch, SSM_FLAT), F32)],
        scratch_shapes=[pltpu.VMEM((rows, 2 * SSM_FLAT), F32), pltpu.VMEM((rows, 2 * SSM_FLAT), F32),
                        pltpu.VMEM((S5_SEQS, 2 * SSM_FLAT), F32)],
        compiler_params=_cparams("arbitrary", "arbitrary"),
        name="s5_prompt",
    )(u3, consts["perm"], consts["perm_t"], lw["bbd"], lw["a_re"], lw["a_im"], lw["cbd"], lw["ssm_d"],
      lw["glu_w"], lw["glu_b"])
    return yb.reshape(batch * seq, SSM_WIDTH), sre, sim


def _s5_step_kernel(u_ref, x0r_ref, x0i_ref, bbd_ref, ar_ref, ai_ref, cbd_ref, d_ref, gw_ref, gb_ref,
                    yb_ref, sre_ref, sim_ref):
    u = u_ref[...]
    bu = _dot(u.astype(BF16), bbd_ref[...])
    ar = ar_ref[...]
    ai = ai_ref[...]
    xr = x0r_ref[...]
    xi = x0i_ref[...]
    nr = ar * xr - ai * xi + bu[:, :SSM_FLAT]
    ni = ar * xi + ai * xr + bu[:, SSM_FLAT:]
    sre_ref[...] = nr
    sim_ref[...] = ni
    xcat = jnp.concatenate([nr, ni], axis=-1).astype(BF16)
    y = _dot(xcat, cbd_ref[...]) + d_ref[...] * u
    yb_ref[...] = _gelu_glu(y, gw_ref, gb_ref).astype(BF16)


def _s5_step(bu, x0r, x0i, lw):
    n = bu.shape[0]
    args = (bu, x0r, x0i, lw["bbd"], lw["a_re"], lw["a_im"], lw["cbd"], lw["ssm_d"], lw["glu_w"], lw["glu_b"])
    return pl.pallas_call(
        _s5_step_kernel,
        out_shape=[jax.ShapeDtypeStruct((n, SSM_WIDTH), BF16), jax.ShapeDtypeStruct((n, SSM_FLAT), F32),
                   jax.ShapeDtypeStruct((n, SSM_FLAT), F32)],
        compiler_params=pltpu.CompilerParams(vmem_limit_bytes=VMEM_LIMIT),
        name="s5_step",
    )(*args)


EXP2_SCALE = ATTN_SCALE * math.log2(math.e)
ATTN_MB = 256


def _attn_prompt_kernel(qlat_ref, qpe_ref, k_ref, wuv_ref, yc_ref, qs_s, m_s, acc_s, *, tq):
    i = pl.program_id(1)
    nr = MLA_HEADS * tq
    lane_head = lax.broadcasted_iota(jnp.int32, (tq, LANES), 1) // QK_ROPE
    for hd in range(MLA_HEADS):
        qs_s[hd * tq:(hd + 1) * tq, 0:KV_RANK] = qlat_ref[:, hd * KV_RANK:(hd + 1) * KV_RANK]
        grp = qpe_ref[:, (hd // 4) * LANES:(hd // 4 + 1) * LANES]
        qs_s[hd * tq:(hd + 1) * tq, KV_RANK:KCAT] = jnp.where(lane_head == hd % 4, grp, jnp.zeros_like(grp))
    m_s[...] = jnp.full((nr, LANES), NEG, F32)
    acc_s[...] = jnp.zeros((nr, 2 * KV_RANK), F32)
    ones = jnp.ones((tq, KV_RANK), BF16)

    ncol = tq // LANES

    def tile(j, masked):
        k0 = pl.multiple_of(j * tq, tq)
        kt = k_ref[pl.ds(k0, tq), :]
        v = jnp.concatenate([kt[:, 0:KV_RANK], ones], axis=-1)
        for blk in range(nr // ATTN_MB):
            rs = slice(blk * ATTN_MB, (blk + 1) * ATTN_MB)
            s = _dot_nt(qs_s[rs, :], kt) * EXP2_SCALE
            if masked:
                qrow = (blk * ATTN_MB) % tq + lax.broadcasted_iota(jnp.int32, (ATTN_MB, tq), 0)
                s = jnp.where(lax.broadcasted_iota(jnp.int32, (ATTN_MB, tq), 1) <= qrow, s, NEG)
            parts = [s[:, c * LANES:(c + 1) * LANES] for c in range(ncol)]
            mx = parts[0]
            for part in parts[1:]:
                mx = jnp.maximum(mx, part)
            m_old = m_s[rs, :]
            m_new = jnp.maximum(m_old, jnp.max(mx, axis=-1, keepdims=True))
            alpha = jnp.exp2(m_old - m_new)
            m_s[rs, :] = m_new
            p = jnp.concatenate([jnp.exp2(part - m_new).astype(BF16) for part in parts], axis=-1)
            acc_s[rs, :] = jnp.concatenate([alpha, alpha], axis=-1) * acc_s[rs, :] + _dot(p, v)

    def off_diag(j, carry):
        tile(j, False)
        return carry

    lax.fori_loop(0, i, off_diag, 0)
    tile(i, True)
    o = acc_s[:, 0:KV_RANK] / acc_s[:, KV_RANK:2 * KV_RANK]
    for j in range(MLA_HEADS // 2):
        pair = jnp.concatenate([o[(2 * j) * tq:(2 * j + 1) * tq], o[(2 * j + 1) * tq:(2 * j + 2) * tq]], axis=-1)
        yc_ref[:, j * 128:(j + 1) * 128] = _dot(pair.astype(BF16), wuv_ref[j]).astype(BF16)


def _attn_prompt(qlat, qpe, kcat, lw, *, batch, seq):
    tq = min(256, seq)
    nr = MLA_HEADS * tq
    yc = pl.pallas_call(
        functools.partial(_attn_prompt_kernel, tq=tq),
        grid=(batch, seq // tq),
        in_specs=[pl.BlockSpec((None, tq, MLA_HEADS * KV_RANK), lambda b, i: (b, i, 0)),
                  pl.BlockSpec((None, tq, MLA_HEADS * QK_ROPE), lambda b, i: (b, i, 0)),
                  pl.BlockSpec((None, seq, KCAT), lambda b, i: (b, 0, 0)),
                  pl.BlockSpec(lw["wuv_bd"].shape, lambda b, i: (0, 0, 0))],
        out_specs=pl.BlockSpec((None, tq, MLA_WIDTH), lambda b, i: (b, i, 0)),
        out_shape=jax.ShapeDtypeStruct((batch, seq, MLA_WIDTH), BF16),
        scratch_shapes=[pltpu.VMEM((nr, KCAT), BF16), pltpu.VMEM((nr, LANES), F32),
                        pltpu.VMEM((nr, 2 * KV_RANK), F32)],
        compiler_params=_cparams("arbitrary", "arbitrary"),
        name="attn_prompt",
    )(qlat.reshape(batch, seq, -1), qpe.reshape(batch, seq, -1), kcat.reshape(batch, seq, KCAT), lw["wuv_bd"])
    return yc.reshape(batch * seq, MLA_WIDTH)


def _attn_sample_kernel(pt_ref, ql_ref, qp_ref, kn_ref, pn_ref, wuv_ref, ckv_hbm, kpet_hbm, yc_ref,
                        cbuf, pbuf, sem, m_s, l_s, acc_s, *, layer, pc, nchunk, page):
    b = pl.program_id(0)
    c = pl.program_id(1)
    step = b * nchunk + c
    nstep = pl.num_programs(0) * nchunk
    slot = step % 2

    def page_copies(bb, cc, sl, p):
        pg = pt_ref[bb, cc * pc + p]
        off = pl.multiple_of(p * page, page)
        return (pltpu.make_async_copy(ckv_hbm.at[layer, pg], cbuf.at[sl, p], sem.at[sl, 0]),
                pltpu.make_async_copy(kpet_hbm.at[layer, pg], pbuf.at[sl, :, pl.ds(off, page)], sem.at[sl, 1]))

    def start_chunk(bb, cc, sl):
        def body(p, carry):
            for cp in page_copies(bb, cc, sl, p):
                cp.start()
            return carry
        lax.fori_loop(0, pc, body, 0, unroll=min(4, pc))

    def wait_chunk(sl):
        pltpu.make_async_copy(ckv_hbm.at[layer, pl.ds(0, pc)], cbuf.at[sl], sem.at[sl, 0]).wait()
        pltpu.make_async_copy(pbuf.at[1 - sl], pbuf.at[sl], sem.at[sl, 1]).wait()

    @pl.when(step == 0)
    def _():
        start_chunk(b, c, slot)

    @pl.when(step + 1 < nstep)
    def _():
        nxt = step + 1
        start_chunk(nxt // nchunk, nxt % nchunk, 1 - slot)

    ql = ql_ref[...]
    qp = qp_ref[...]

    @pl.when(c == 0)
    def _():
        kn = kn_ref[...].astype(BF16).astype(F32)
        pn = pn_ref[...].astype(BF16).astype(F32)
        s_new = (jnp.sum(ql.astype(F32) * kn, axis=-1, keepdims=True)
                 + jnp.sum(qp.astype(F32) * pn, axis=-1, keepdims=True)) * ATTN_SCALE
        m_s[...] = s_new
        l_s[...] = jnp.ones_like(l_s)
        acc_s[...] = jnp.broadcast_to(kn, acc_s.shape)

    wait_chunk(slot)

    kc = cbuf[slot].reshape(pc * page, KV_RANK).astype(BF16)
    kpt = pbuf[slot].astype(BF16)
    s = (_dot_nt(ql, kc) + _dot(qp, kpt)) * ATTN_SCALE
    m_old = m_s[...]
    m_new = jnp.maximum(m_old, jnp.max(s, axis=-1, keepdims=True))
    alpha = jnp.exp(m_old - m_new)
    p = jnp.exp(s - m_new)
    l_s[...] = alpha * l_s[...] + jnp.sum(p, axis=-1, keepdims=True)
    acc_s[...] = alpha * acc_s[...] + _dot(p.astype(BF16), kc)
    m_s[...] = m_new

    @pl.when(c == nchunk - 1)
    def _():
        o = (acc_s[...] / l_s[...]).astype(BF16)
        yfull = _dot(o, wuv_ref[...])
        sel = lax.broadcasted_iota(jnp.int32, yfull.shape, 1) // V_DIM == lax.broadcasted_iota(
            jnp.int32, yfull.shape, 0)
        yc_ref[...] = jnp.sum(jnp.where(sel, yfull, 0.0), axis=0, keepdims=True).astype(BF16)


SAMPLE_CHUNK_ROWS = 16384


def _attn_sample(qlat, qpe, ckv_new, kpe_new, page_table, cache_ckv, cache_kpet, lw, *, layer):
    n, npages = page_table.shape
    page = cache_ckv.shape[2]
    assert page % LANES == 0
    pc = max(1, SAMPLE_CHUNK_ROWS // page)
    while npages % pc:
        pc -= 1
    nchunk = npages // pc
    ql = qlat.reshape(n, MLA_HEADS, KV_RANK)
    qp = qpe.reshape(n, MLA_HEADS, QK_ROPE)
    grid_spec = pltpu.PrefetchScalarGridSpec(
        num_scalar_prefetch=1,
        grid=(n, nchunk),
        in_specs=[pl.BlockSpec((None, MLA_HEADS, KV_RANK), lambda b, c, pt: (b, 0, 0)),
                  pl.BlockSpec((None, MLA_HEADS, QK_ROPE), lambda b, c, pt: (b, 0, 0)),
                  pl.BlockSpec((None, 1, KV_RANK), lambda b, c, pt: (b, 0, 0)),
                  pl.BlockSpec((None, 1, QK_ROPE), lambda b, c, pt: (b, 0, 0)),
                  pl.BlockSpec(lw["wuv_flat"].shape, lambda b, c, pt: (0, 0)),
                  pl.BlockSpec(memory_space=pl.ANY),
                  pl.BlockSpec(memory_space=pl.ANY)],
        out_specs=pl.BlockSpec((None, 1, MLA_WIDTH), lambda b, c, pt: (b, 0, 0)),
        scratch_shapes=[pltpu.VMEM((2, pc, page, KV_RANK), F32), pltpu.VMEM((2, QK_ROPE, pc * page), F32),
                        pltpu.SemaphoreType.DMA((2, 2)),
                        pltpu.VMEM((MLA_HEADS, 1), F32), pltpu.VMEM((MLA_HEADS, 1), F32),
                        pltpu.VMEM((MLA_HEADS, KV_RANK), F32)],
    )
    yc = pl.pallas_call(
        functools.partial(_attn_sample_kernel, layer=layer, pc=pc, nchunk=nchunk, page=page),
        grid_spec=grid_spec,
        out_shape=jax.ShapeDtypeStruct((n, 1, MLA_WIDTH), BF16),
        compiler_params=_cparams("arbitrary", "arbitrary"),
        name="attn_sample",
    )(page_table, ql, qp, ckv_new.reshape(n, 1, KV_RANK), kpe_new.reshape(n, 1, QK_ROPE), lw["wuv_flat"],
      cache_ckv, cache_kpet)
    return yc.reshape(n, MLA_WIDTH)


def _post_kernel(x_ref, ya_ref, yb_ref, yc_ref, wo_ref, g1_ref, n2_ref, sc2_ref, sh2_ref, wrhl_ref, br_ref,
                 ltri_ref, x1_ref, h2_ref, route_ref, routet_ref, cnt_ref, run_s):
    i = pl.program_id(0)

    @pl.when(i == 0)
    def _():
        run_s[...] = jnp.zeros_like(run_s)

    y = (_dot(ya_ref[...], wo_ref[0:256, :]) + _dot(yb_ref[...], wo_ref[256:512, :])
         + _dot(yc_ref[...], wo_ref[512:1024, :]))
    x1 = x_ref[...] + g1_ref[...] * y
    x1_ref[...] = x1
    h2 = _rms(x1) * n2_ref[...] * (1.0 + sc2_ref[...]) + sh2_ref[...]
    h2_ref[...] = _pack_rows(h2)

    hh = h2.astype(BF16)
    hl = (h2 - hh.astype(F32)).astype(BF16)
    d1 = _dot(hh, wrhl_ref[...])
    logits = d1[:, 0:LANES] + d1[:, LANES:2 * LANES] + _dot(hl, wrhl_ref[:, 0:LANES]) + br_ref[...]
    tm = logits.shape[0]
    lane = lax.broadcasted_iota(jnp.int32, (tm, LANES), 1).astype(F32)
    big = float(LANES)

    gl = jnp.where(lane < N_GROUPS, logits, NEG)
    gmax = jnp.max(gl, axis=-1, keepdims=True)
    gidx = jnp.min(jnp.where(gl == gmax, lane, big), axis=-1, keepdims=True)
    gden = jnp.sum(jnp.where(lane < N_GROUPS, jnp.exp(gl - gmax), 0.0), axis=-1, keepdims=True)
    gprob = 1.0 / gden

    lo = N_GROUPS + EXPERTS_PER_GROUP * gidx
    el = jnp.where((lane >= lo) & (lane < lo + EXPERTS_PER_GROUP), logits, NEG)
    m1 = jnp.max(el, axis=-1, keepdims=True)
    i1 = jnp.min(jnp.where(el == m1, lane, big), axis=-1, keepdims=True)
    el2 = jnp.where(lane == i1, NEG, el)
    m2 = jnp.max(el2, axis=-1, keepdims=True)
    i2 = jnp.min(jnp.where(el2 == m2, lane, big), axis=-1, keepdims=True)
    e21 = jnp.exp(m2 - m1)
    w1 = gprob / (1.0 + e21)
    w2 = gprob * e21 / (1.0 + e21)
    e1 = i1 - N_GROUPS
    e2 = i2 - N_GROUPS

    oh1 = lane == e1
    oh2 = lane == e2
    both = jnp.where(oh1, 1.0, 0.0) + jnp.where(oh2, 1.0, 0.0)
    tot = _dot(ltri_ref[...], both.astype(BF16)) + run_s[...]
    r1 = jnp.sum(jnp.where(oh1, tot, 0.0), axis=-1, keepdims=True)
    r2 = jnp.sum(jnp.where(oh2, tot, 0.0), axis=-1, keepdims=True)
    run = run_s[...] + jnp.sum(both, axis=0, keepdims=True)
    run_s[...] = run
    cnt_ref[...] = jnp.broadcast_to(run, cnt_ref.shape)

    route = jnp.where(lane == 0, e1, jnp.where(lane == 1, e2, jnp.where(lane == 2, r1, jnp.where(
        lane == 3, r2, jnp.where(lane == 4, w1, jnp.where(lane == 5, w2, 0.0))))))
    route_ref[...] = route[:, 0:SUBLANES]
    routet_ref[...] = jnp.transpose(route)[0:SUBLANES, :]


def _post(x, ya, yb, yc, mod3, lw, consts, *, seq, tm, per_seq):
    T = x.shape[0]
    nt = T // tm
    if per_seq:
        per = seq // tm

        def modspec(k):
            return pl.BlockSpec((None, 1, D_MODEL), lambda i: (i // per, 0, k))
    else:
        def modspec(k):
            return pl.BlockSpec((tm, D_MODEL), lambda i: (i, k))

    def full(a):
        nd = a.ndim
        return pl.BlockSpec(a.shape, lambda i: (0,) * nd)

    def rows(w):
        return pl.BlockSpec((tm, w), lambda i: (i, 0))

    ltri = consts["ltri"][tm]
    return pl.pallas_call(
        _post_kernel,
        grid=(nt,),
        in_specs=[rows(D_MODEL), rows(A_WIDTH), rows(SSM_WIDTH), rows(MLA_WIDTH), full(lw["w_out"]),
                  modspec(2), full(lw["norm2_g"]), modspec(4), modspec(3),
                  full(lw["wr_hl"]), full(lw["br"]), full(ltri)],
        out_specs=[rows(D_MODEL), rows(HALF), rows(SUBLANES), pl.BlockSpec((SUBLANES, tm), lambda i: (0, i)),
                   pl.BlockSpec((SUBLANES, LANES), lambda i: (0, 0))],
        out_shape=[jax.ShapeDtypeStruct((T, D_MODEL), F32), jax.ShapeDtypeStruct((T, HALF), jnp.uint32),
                   jax.ShapeDtypeStruct((T, SUBLANES), F32), jax.ShapeDtypeStruct((SUBLANES, T), F32),
                   jax.ShapeDtypeStruct((SUBLANES, LANES), F32)],
        scratch_shapes=[pltpu.VMEM((1, LANES), F32)],
        compiler_params=_cparams("arbitrary"),
        name="post",
    )(x, ya, yb, yc, lw["w_out"], mod3, lw["norm2_g"], mod3, mod3, lw["wr_hl"], lw["br"], ltri)


def _idx_copy(dest_hbm, idx_s, isem, step, sl):
    n = dest_hbm.shape[1]
    return pltpu.make_async_copy(dest_hbm.at[step], idx_s.at[pl.ds(pl.multiple_of(sl * n, n), n)], isem.at[sl])


def _stage_indices(dest_hbm, idx_s, isem):
    i = pl.program_id(0)
    n = pl.num_programs(0)
    slot = i % 2

    def cp(step, sl):
        return _idx_copy(dest_hbm, idx_s, isem, step, sl)

    @pl.when(i == 0)
    def _():
        cp(0, 0).start()

    cp(i, slot).wait()

    @pl.when(i + 1 < n)
    def _():
        cp(i + 1, 1 - slot).start()

    return slot


HALF = D_MODEL // 2
HI_MASK = 0xFFFF0000


def _pack_rows(x):
    u = lax.bitcast_convert_type(x.astype(BF16).astype(F32), jnp.uint32)
    return (u[:, :HALF] >> 16) | (u[:, HALF:] & jnp.uint32(HI_MASK))


def _unpack_rows(w):
    lo = lax.bitcast_convert_type(w << 16, F32)
    hi = lax.bitcast_convert_type(w & jnp.uint32(HI_MASK), F32)
    return lo, hi


def _dispatch_kernel(dest_hbm, h_ref, xin_hbm, xbuf_hbm, idx_s, isem, hs, rsem, *, tm, nt):
    del xin_hbm
    i = pl.program_id(0)
    slot = _stage_indices(dest_hbm, idx_s, isem)
    hs[slot] = h_ref[...]

    src = hs.at[slot]
    sem = rsem.at[slot]
    base = slot * (2 * tm)

    def issue(g, carry):
        r0 = pl.multiple_of(g * SUBLANES, SUBLANES)
        grp = src.at[pl.ds(r0, SUBLANES)]
        for k in range(SUBLANES):
            for half in range(2):
                d = idx_s[base + half * tm + r0 + k]
                pltpu.make_async_copy(grp.at[pl.ds(k, 1)], xbuf_hbm.at[pl.ds(d, 1)], sem).start(priority=k % 2)
        return carry

    lax.fori_loop(0, tm // SUBLANES, issue, 0)

    def drain(sl):
        for _ in range(2):
            pltpu.make_async_copy(hs.at[sl], xbuf_hbm.at[pl.ds(0, tm)], rsem.at[sl]).wait()

    @pl.when(i > 0)
    def _():
        drain(1 - slot)

    @pl.when(i == nt - 1)
    def _():
        drain(slot)


def _dispatch(h2p, dest2, n_rows, *, tm):
    T = h2p.shape[0]
    nt = T // tm
    xzero = jnp.zeros((n_rows, HALF), jnp.uint32)
    return pl.pallas_call(
        functools.partial(_dispatch_kernel, tm=tm, nt=nt),
        grid=(nt,),
        in_specs=[pl.BlockSpec(memory_space=pl.ANY), pl.BlockSpec((tm, HALF), lambda i: (i, 0)),
                  pl.BlockSpec(memory_space=pl.ANY)],
        out_specs=pl.BlockSpec(memory_space=pl.ANY),
        out_shape=jax.ShapeDtypeStruct((n_rows, HALF), jnp.uint32),
        scratch_shapes=[pltpu.SMEM((4 * tm,), jnp.int32), pltpu.SemaphoreType.DMA((2,)),
                        pltpu.VMEM((2, tm, HALF), jnp.uint32), pltpu.SemaphoreType.DMA((2,))],
        input_output_aliases={2: 0},
        compiler_params=_cparams("arbitrary"),
        name="moe_dispatch",
    )(dest2, h2p, xzero)


def _ffn_kernel(be_ref, nu_ref, x_ref, wg_ref, wu_ref, wd_ref, y_ref, wg_s, wu_s, wd_s):
    i = pl.program_id(0)

    @pl.when((i == 0) | (be_ref[i] != be_ref[jnp.maximum(i - 1, 0)]))
    def _():
        wg_s[...] = wg_ref[...].astype(BF16)
        wu_s[...] = wu_ref[...].astype(BF16)
        wd_s[...] = wd_ref[...].astype(BF16)

    @pl.when(i < nu_ref[0])
    def _():
        lo, hi = _unpack_rows(x_ref[...])
        x = jnp.concatenate([lo, hi], axis=-1).astype(BF16)
        g = _dot(x, wg_s[...])
        u = _dot(x, wu_s[...])
        mid = (g * jax.nn.sigmoid(g) * u).astype(BF16)
        y_ref[...] = _pack_rows(_dot(mid, wd_s[...]))

    @pl.when(i >= nu_ref[0])
    def _():
        y_ref[...] = jnp.zeros_like(y_ref)


def _ffn(xbuf, block_e, n_used, w_gate, w_up, w_down, *, layer, blk):
    n_rows = xbuf.shape[0]
    nb = n_rows // blk
    grid_spec = pltpu.PrefetchScalarGridSpec(
        num_scalar_prefetch=2,
        grid=(nb,),
        in_specs=[pl.BlockSpec((blk, HALF), lambda i, be, nu: (i, 0)),
                  pl.BlockSpec((None, None, D_MODEL, D_EXPERT), lambda i, be, nu: (layer, be[i], 0, 0)),
                  pl.BlockSpec((None, None, D_MODEL, D_EXPERT), lambda i, be, nu: (layer, be[i], 0, 0)),
                  pl.BlockSpec((None, None, D_EXPERT, D_MODEL), lambda i, be, nu: (layer, be[i], 0, 0))],
        out_specs=pl.BlockSpec((blk, HALF), lambda i, be, nu: (i, 0)),
        scratch_shapes=[pltpu.VMEM((D_MODEL, D_EXPERT), BF16), pltpu.VMEM((D_MODEL, D_EXPERT), BF16),
                        pltpu.VMEM((D_EXPERT, D_MODEL), BF16)],
    )
    return pl.pallas_call(
        _ffn_kernel,
        grid_spec=grid_spec,
        out_shape=jax.ShapeDtypeStruct((n_rows, HALF), jnp.uint32),
        compiler_params=_cparams("arbitrary"),
        name="moe_ffn",
    )(block_e, n_used, xbuf, w_gate, w_up, w_down)


def _combine_kernel(dest_hbm, x1_ref, g2_ref, route_ref, fg_ref, ybuf_hbm, out_ref, idx_s, isem, ysc, rsem,
                    *, tm, nt, final):
    i = pl.program_id(0)
    slot = i % 2

    def idx_copy(step, sl):
        return _idx_copy(dest_hbm, idx_s, isem, step, sl)

    def issue_tile(sl):
        sem = rsem.at[sl]
        base = sl * (2 * tm)

        def issue(g, carry):
            r0 = pl.multiple_of(g * SUBLANES, SUBLANES)
            for half in range(2):
                grp = ysc.at[sl, half, pl.ds(r0, SUBLANES)]
                for k in range(SUBLANES):
                    d = idx_s[base + half * tm + r0 + k]
                    pltpu.make_async_copy(ybuf_hbm.at[pl.ds(d, 1)], grp.at[pl.ds(k, 1)], sem).start(priority=k % 2)
            return carry

        lax.fori_loop(0, tm // SUBLANES, issue, 0)

    @pl.when(i == 0)
    def _():
        idx_copy(0, 0).start()
        idx_copy(0, 0).wait()
        issue_tile(0)
        if nt > 1:
            idx_copy(1, 1).start()

    if nt > 1:
        @pl.when(i + 1 < nt)
        def _():
            idx_copy(i + 1, 1 - slot).wait()
            issue_tile(1 - slot)

            @pl.when(i + 2 < nt)
            def _():
                idx_copy(i + 2, slot).start()

    for k in range(2):
        pltpu.make_async_copy(ybuf_hbm.at[pl.ds(0, tm)], ysc.at[slot, k], rsem.at[slot]).wait()

    route = route_ref[...]
    w1 = route[:, 4:5]
    w2 = route[:, 5:6]
    lo1, hi1 = _unpack_rows(ysc[slot, 0])
    lo2, hi2 = _unpack_rows(ysc[slot, 1])
    ff = jnp.concatenate([lo1 * w1 + lo2 * w2, hi1 * w1 + hi2 * w2], axis=-1)
    x2 = x1_ref[...] + g2_ref[...] * ff
    if final:
        x2 = _rms(x2) * fg_ref[...]
    out_ref[...] = x2


def _combine(x1, mod3, route, ybuf, dest2, final_g, *, seq, tm, per_seq, final):
    T = x1.shape[0]
    if per_seq:
        per = seq // tm
        g2spec = pl.BlockSpec((None, 1, D_MODEL), lambda i: (i // per, 0, 5))
    else:
        g2spec = pl.BlockSpec((tm, D_MODEL), lambda i: (i, 5))
    return pl.pallas_call(
        functools.partial(_combine_kernel, tm=tm, nt=T // tm, final=final),
        grid=(T // tm,),
        in_specs=[pl.BlockSpec(memory_space=pl.ANY), pl.BlockSpec((tm, D_MODEL), lambda i: (i, 0)), g2spec,
                  pl.BlockSpec((tm, SUBLANES), lambda i: (i, 0)), pl.BlockSpec((1, D_MODEL), lambda i: (0, 0)),
                  pl.BlockSpec(memory_space=pl.ANY)],
        out_specs=pl.BlockSpec((tm, D_MODEL), lambda i: (i, 0)),
        out_shape=jax.ShapeDtypeStruct((T, D_MODEL), F32),
        scratch_shapes=[pltpu.SMEM((4 * tm,), jnp.int32), pltpu.SemaphoreType.DMA((2,)),
                        pltpu.VMEM((2, 2, tm, HALF), jnp.uint32), pltpu.SemaphoreType.DMA((2,))],
        compiler_params=_cparams("arbitrary"),
        name="moe_combine",
    )(dest2, x1, mod3, route, final_g, ybuf)


def _moe(x1, h2, route, route_t, counts, mod3, experts, final_g, *, layer, seq, tm, per_seq, blk, final):
    T = x1.shape[0]
    cnt = counts[0, :N_EXPERTS].astype(jnp.int32)
    padded = ((cnt + blk - 1) // blk) * blk
    pend = jnp.cumsum(padded)
    pstart = pend - padded
    nb = (2 * T + blk - 1) // blk + N_EXPERTS
    eid = route_t[0:2].astype(jnp.int32)
    rank = route_t[2:4].astype(jnp.int32)
    sel = eid[None] == jnp.arange(N_EXPERTS, dtype=jnp.int32)[:, None, None]
    dest = rank + jnp.sum(jnp.where(sel, pstart[:, None, None], 0), axis=0)
    dest2 = jnp.transpose(dest.reshape(2, T // tm, tm), (1, 0, 2)).reshape(T // tm, 2 * tm)
    starts = jnp.arange(nb, dtype=jnp.int32) * blk
    block_e = jnp.minimum(jnp.sum((pend[None, :] <= starts[:, None]).astype(jnp.int32), axis=1), N_EXPERTS - 1)
    n_used = (pend[-1:] // blk).astype(jnp.int32)
    xbuf = _dispatch(h2, dest2, nb * blk, tm=tm)
    ybuf = _ffn(xbuf, block_e, n_used, *experts, layer=layer, blk=blk)
    return _combine(x1, mod3, route, ybuf, dest2, final_g, seq=seq, tm=tm, per_seq=per_seq, final=final)


def _rope_tables(pos0, n):
    pos = (pos0 + jnp.arange(n)).astype(F32)
    inv = jnp.power(ROPE_THETA, -jnp.arange(0, QK_ROPE, 2, dtype=F32) / QK_ROPE)
    ang = pos[:, None] * inv[None, :]
    cos, sin = jnp.cos(ang), jnp.sin(ang)
    c32 = jnp.concatenate([cos, cos], axis=-1)
    s32 = jnp.concatenate([-sin, sin], axis=-1)
    return {"cq": jnp.tile(c32, (1, MLA_HEADS)), "sq": jnp.tile(s32, (1, MLA_HEADS)),
            "ck": jnp.tile(c32, (1, 4)), "sk": jnp.tile(s32, (1, 4))}


def _pair_blockdiag(w):
    z = jnp.zeros_like(w[:, 0])
    top = jnp.concatenate([w[:, 0], z], axis=-1)
    bot = jnp.concatenate([z, w[:, 1]], axis=-1)
    return jnp.concatenate([top, bot], axis=-2)


def _prep_layer(l, p):
    swap = jnp.concatenate([jnp.arange(16, 32), jnp.arange(0, 16)])
    w_in = p["w_in"][l]
    o5 = 2 * A_WIDTH + SSM_WIDTH + Q_RANK + KV_RANK
    kpe_w = w_in[:, o5:o5 + QK_ROPE]
    w_in_p = jnp.concatenate([w_in[:, :o5], jnp.tile(kpe_w, (1, 4)), jnp.tile(kpe_w[:, swap], (1, 4))], axis=-1)

    sp_w = jnp.tril(p["sp_w"][l])
    gm_wcat = jnp.transpose(sp_w, (1, 0, 2)).reshape(CHUNK, A_HEADS * CHUNK)
    gm_bt = jnp.repeat(jnp.transpose(p["sp_b"][l]), A_HEAD_DIM, axis=1)
    gm_w0 = jnp.repeat(p["sp_w"][l][:, 0, 0], A_HEAD_DIM)[None, :]
    gm_b0 = jnp.repeat(p["sp_b"][l][:, 0], A_HEAD_DIM)[None, :]

    w_uq = p["w_uq"][l].reshape(Q_RANK, MLA_HEADS, QK_NOPE + QK_ROPE)
    pe = w_uq[:, :, QK_NOPE:]
    wq = jnp.concatenate([w_uq[:, :, :QK_NOPE].reshape(Q_RANK, -1), pe.reshape(Q_RANK, -1),
                          pe[:, :, swap].reshape(Q_RANK, -1)], axis=-1)
    w_uk = jnp.transpose(p["w_uk"][l], (1, 2, 0))
    wuk_bd = _pair_blockdiag(w_uk.reshape(MLA_HEADS // 2, 2, QK_NOPE, KV_RANK))
    w_uv = jnp.transpose(p["w_uv"][l], (1, 0, 2))
    wuv_bd = _pair_blockdiag(w_uv.reshape(MLA_HEADS // 2, 2, KV_RANK, V_DIM))
    wuv_flat = p["w_uv"][l].reshape(KV_RANK, MLA_HEADS * V_DIM)

    dt = jnp.exp(p["ssm_log_step"][l])[:, None]
    lam_re, lam_im = p["ssm_lam_re"][l], p["ssm_lam_im"][l]
    mag = jnp.exp(lam_re * dt)
    abar_re = mag * jnp.cos(lam_im * dt)
    abar_im = mag * jnp.sin(lam_im * dt)
    den = lam_re * lam_re + lam_im * lam_im
    nr = abar_re - 1.0
    fr = (nr * lam_re + abar_im * lam_im) / den
    fi = (abar_im * lam_re - nr * lam_im) / den
    b_re, b_im = p["ssm_b_re"][l], p["ssm_b_im"][l]
    bbar_re = fr[..., None] * b_re - fi[..., None] * b_im
    bbar_im = fr[..., None] * b_im + fi[..., None] * b_re
    eye = jnp.eye(SSM_GROUPS, dtype=F32)

    def bd_in(bb):
        return jnp.einsum("gpc,gh->gchp", bb, eye).reshape(SSM_WIDTH, SSM_FLAT)

    def bd_out(cc):
        return jnp.einsum("gcp,gh->hpgc", cc, eye).reshape(SSM_FLAT, SSM_WIDTH)

    bbd = jnp.concatenate([bd_in(bbar_re), bd_in(bbar_im)], axis=-1)
    cbd = jnp.concatenate([bd_out(p["ssm_c_re"][l]), -bd_out(p["ssm_c_im"][l])], axis=0)

    wr = jnp.zeros((D_MODEL, LANES), F32)
    wr = wr.at[:, 0:N_GROUPS].set(p["router_g_w"][l]).at[:, N_GROUPS:N_GROUPS + N_EXPERTS].set(p["router_e_w"][l])
    br = jnp.zeros((1, LANES), F32)
    br = br.at[0, 0:N_GROUPS].set(p["router_g_b"][l]).at[0, N_GROUPS:N_GROUPS + N_EXPERTS].set(p["router_e_b"][l])
    wr_hi = wr.astype(BF16)
    wr_lo = (wr - wr_hi.astype(F32)).astype(BF16)

    return {
        "norm1_g": p["norm1_g"][l][None, :], "norm2_g": p["norm2_g"][l][None, :],
        "w_in": w_in_p.astype(BF16),
        "gm_wcat": gm_wcat.astype(BF16), "gm_bt": gm_bt, "gm_w0": gm_w0, "gm_b0": gm_b0,
        "q_norm_g": p["q_norm_g"][l][None, :], "wq": wq.astype(BF16), "wuk_bd": wuk_bd.astype(BF16),
        "kv_norm_g": p["kv_norm_g"][l][None, :], "wuv_bd": wuv_bd.astype(BF16), "wuv_flat": wuv_flat.astype(BF16),
        "bbd": bbd.astype(BF16), "cbd": cbd.astype(BF16),
        "a_re": abar_re.reshape(1, SSM_FLAT), "a_im": abar_im.reshape(1, SSM_FLAT),
        "ssm_d": p["ssm_d"][l][None, :], "glu_w": p["glu_w"][l].astype(BF16), "glu_b": p["glu_b"][l][None, :],
        "w_out": p["w_out"][l].astype(BF16),
        "wr_hl": jnp.concatenate([wr_hi, wr_lo], axis=-1), "br": br,
    }


def _consts(tms):
    rows = S5_SEQS * S5_SUB
    r = jnp.arange(rows)
    src = (r % S5_SEQS) * S5_SUB + r // S5_SEQS
    perm = (jnp.arange(rows)[None, :] == src[:, None]).astype(BF16)
    ltri = {tm: (jnp.arange(tm)[None, :] < jnp.arange(tm)[:, None]).astype(BF16) for tm in tms}
    return {"perm": perm, "perm_t": jnp.transpose(perm), "ltri": ltri}


def kernel(x_prompt, x_sample, cache_ckv, cache_kpe, state_ssm_re, state_ssm_im, page_table, c_prompt, c_sample, norm1_g, norm2_g, ada_w, ada_b, w_in, sp_w, sp_b, ssm_lam_re, ssm_lam_im, ssm_b_re, ssm_b_im, ssm_c_re, ssm_c_im, ssm_d, ssm_log_step, glu_w, glu_b, q_norm_g, w_uq, kv_norm_g, w_uk, w_uv, w_out, router_g_w, router_g_b, router_e_w, router_e_b, exp_w_gate, exp_w_up, exp_w_down, final_norm_g):
    params = dict(norm1_g=norm1_g, norm2_g=norm2_g, w_in=w_in, sp_w=sp_w, sp_b=sp_b, ssm_lam_re=ssm_lam_re,
                  ssm_lam_im=ssm_lam_im, ssm_b_re=ssm_b_re, ssm_b_im=ssm_b_im, ssm_c_re=ssm_c_re, ssm_c_im=ssm_c_im,
                  ssm_d=ssm_d, ssm_log_step=ssm_log_step, glu_w=glu_w, glu_b=glu_b, q_norm_g=q_norm_g, w_uq=w_uq,
                  kv_norm_g=kv_norm_g, w_uk=w_uk, w_uv=w_uv, w_out=w_out, router_g_w=router_g_w,
                  router_g_b=router_g_b, router_e_w=router_e_w, router_e_b=router_e_b, exp_w_gate=exp_w_gate,
                  exp_w_up=exp_w_up, exp_w_down=exp_w_down)
    depth = w_in.shape[0]
    nb_p, seq, _ = x_prompt.shape
    nb_s = x_sample.shape[0]
    assert x_sample.shape[1] == 1 and nb_p % S5_SEQS == 0 and seq % CHUNK == 0
    past_len = page_table.shape[1] * cache_ckv.shape[2]
    tm_p = min(512, seq)
    tm_s = nb_s
    blk_p = min(512, seq)
    blk_s = 128
    consts = _consts({tm_p, tm_s})
    final_g = final_norm_g[None, :]

    mod = _modulation(jnp.concatenate([c_prompt, c_sample], axis=0), ada_w, ada_b)
    rope_p = _rope_tables(0, seq)
    rope_s = {k: jnp.broadcast_to(v, (nb_s, v.shape[1])) for k, v in _rope_tables(past_len, 1).items()}

    cache_kpet = jnp.swapaxes(cache_kpe, 2, 3)
    experts = (exp_w_gate, exp_w_up, exp_w_down)

    xp = x_prompt.reshape(nb_p * seq, D_MODEL)
    xs = x_sample.reshape(nb_s, D_MODEL)
    outs = {k: [] for k in ("p_ckv", "p_kpe", "p_sre", "p_sim", "s_ckv", "s_kpe", "s_sre", "s_sim", "s_v")}
    for l in range(depth):
        lw = _prep_layer(l, params)
        last = l == depth - 1
        mod_p = mod[l, :nb_p].reshape(nb_p, 1, 6 * D_MODEL)
        ya, bu, qlat, qpe, ckv, kpe, kcat, _ = _proj(xp, mod_p, lw, rope_p, seq=seq, tm=tm_p, chunked=True)
        yb, sre, sim = _s5_prompt(bu, lw, consts, batch=nb_p, seq=seq)
        yc = _attn_prompt(qlat, qpe, kcat, lw, batch=nb_p, seq=seq)
        x1, h2, route, route_t, counts = _post(xp, ya, yb, yc, mod_p, lw, consts, seq=seq, tm=tm_p, per_seq=True)
        xp = _moe(x1, h2, route, route_t, counts, mod_p, experts, final_g, layer=l, seq=seq, tm=tm_p, per_seq=True,
                  blk=blk_p, final=last)
        outs["p_ckv"].append(ckv.reshape(nb_p, seq, KV_RANK))
        outs["p_kpe"].append(kpe.reshape(nb_p, seq, QK_ROPE))
        outs["p_sre"].append(sre.reshape(nb_p, SSM_GROUPS, SSM_STATE))
        outs["p_sim"].append(sim.reshape(nb_p, SSM_GROUPS, SSM_STATE))

        mod_s = mod[l, nb_p:]
        ya, bu, qlat, qpe, ckv, kpe, kcat, av = _proj(xs, mod_s, lw, rope_s, seq=1, tm=tm_s, chunked=False)
        yb, sre, sim = _s5_step(bu, state_ssm_re[l].reshape(nb_s, SSM_FLAT), state_ssm_im[l].reshape(nb_s, SSM_FLAT),
                                lw)
        yc = _attn_sample(qlat, qpe, ckv, kpe, page_table, cache_ckv, cache_kpet, lw, layer=l)
        x1, h2, route, route_t, counts = _post(xs, ya, yb, yc, mod_s, lw, consts, seq=1, tm=tm_s, per_seq=False)
        xs = _moe(x1, h2, route, route_t, counts, mod_s, experts, final_g, layer=l, seq=1, tm=tm_s, per_seq=False,
                  blk=blk_s, final=last)
        outs["s_ckv"].append(ckv.reshape(nb_s, 1, KV_RANK))
        outs["s_kpe"].append(kpe.reshape(nb_s, 1, QK_ROPE))
        outs["s_sre"].append(sre.reshape(nb_s, SSM_GROUPS, SSM_STATE))
        outs["s_sim"].append(sim.reshape(nb_s, SSM_GROUPS, SSM_STATE))
        outs["s_v"].append(av.reshape(nb_s, 1, A_WIDTH))

    return (xp.reshape(nb_p, seq, D_MODEL), xs.reshape(nb_s, 1, D_MODEL),
            jnp.stack(outs["p_ckv"]), jnp.stack(outs["p_kpe"]), jnp.stack(outs["p_sre"]), jnp.stack(outs["p_sim"]),
            jnp.stack(outs["s_ckv"]), jnp.stack(outs["s_kpe"]), jnp.stack(outs["s_sre"]), jnp.stack(outs["s_sim"]),
            jnp.stack(outs["s_v"]))
```
